```python
import math
import jax, jax.numpy as jnp
from jax import lax
import numpy as np

D_MODEL = 1024
BATCH = 8
SEQ = 8192
DEPTH = 2

GDN_HEADS = D_MODEL // 128
GDN_DK = 128
GDN_DV = 256
GDN_QK_W = GDN_HEADS * GDN_DK
GDN_V_W = GDN_HEADS * GDN_DV
GDN_CONV_W = 2 * GDN_QK_W + GDN_V_W
GDN_IN_W = GDN_CONV_W + GDN_V_W + 2 * GDN_HEADS
CONV_K = 4
CHUNK = 64
DIL_GROUPS = ((128, 1), (512, 4), (2048, 16))
N_GROUPS = 3
HEADS_PER_GROUP = 4
HEAD_DIM = 128
ATT_Q_W = N_GROUPS * HEADS_PER_GROUP * HEAD_DIM
ATT_OUT_W = HEADS_PER_GROUP * HEAD_DIM
ATT_IN_W = ATT_Q_W + ATT_OUT_W
ATT_BLOCK = 128
ROPE_THETA = 10000.0
RMS_EPS = 1e-6
LN_EPS = 1e-5

kernel_name = 'yoco_gated_deltanet_dilated_swa'


def layer_norm(x, g, b):
    xf = x.astype(jnp.float32)
    mu = jnp.mean(xf, -1, keepdims=True)
    var = jnp.mean(jnp.square(xf - mu), -1, keepdims=True)
    return ((xf - mu) * lax.rsqrt(var + LN_EPS) * g.astype(jnp.float32) + b.astype(jnp.float32)).astype(x.dtype)


def l2norm(x):
    return x * lax.rsqrt(jnp.sum(x * x, -1, keepdims=True) + RMS_EPS)


def rope_tables(seq_len):
    inv = 1.0 / (ROPE_THETA ** (jnp.arange(0, HEAD_DIM, 2, dtype=jnp.float32) / HEAD_DIM))
    ang = jnp.arange(seq_len, dtype=jnp.float32)[:, None] * inv[None, :]
    ang = jnp.concatenate([ang, ang], -1)
    return jnp.cos(ang), jnp.sin(ang)


def apply_rope(x, cos, sin):
    half = HEAD_DIM // 2
    rot = jnp.concatenate([-x[..., half:], x[..., :half]], -1)
    shape = (1, x.shape[1]) + (1,) * (x.ndim - 3) + (HEAD_DIM,)
    return x * cos.reshape(shape) + rot * sin.reshape(shape)


def causal_depthwise_conv(x, w):
    return lax.conv_general_dilated(
        x, w[:, None, :].astype(x.dtype), window_strides=(1,), padding=[(CONV_K - 1, 0)],
        dimension_numbers=('NWC', 'WIO', 'NWC'), feature_group_count=x.shape[-1])


def chunk_gated_delta_rule(q, k, v, g, beta):
    B, S, H, DK = q.shape
    DV = v.shape[-1]
    N = S // CHUNK

    def chunks(a):
        a = a.reshape((B, N, CHUNK, H) + a.shape[3:])
        return jnp.moveaxis(a, 3, 1)

    q, k, v, beta = chunks(q), chunks(k), chunks(v), chunks(beta)
    g = jnp.cumsum(chunks(g), -1)
    idx = jnp.arange(CHUNK)
    incl = idx[:, None] >= idx[None, :]
    strict = idx[:, None] > idx[None, :]
    decay = jnp.where(incl, jnp.exp(jnp.where(incl, g[..., :, None] - g[..., None, :], 0.0)), 0.0)
    kb = k * beta[..., None]
    a_mat = jnp.where(strict, jnp.einsum('bhnid,bhnjd->bhnij', kb, k) * decay, 0.0)
    eye = jnp.eye(CHUNK, dtype=jnp.float32)
    t_mat = lax.linalg.triangular_solve(eye + a_mat, jnp.broadcast_to(eye, a_mat.shape),
                                        left_side=True, lower=True, unit_diagonal=True)
    u = jnp.einsum('bhnij,bhnjv->bhniv', t_mat, v * beta[..., None])
    w = jnp.einsum('bhnij,bhnjk->bhnik', t_mat, kb * jnp.exp(g)[..., None])
    qk = jnp.einsum('bhnid,bhnjd->bhnij', q, k) * decay
    qg = q * jnp.exp(g)[..., None]
    g_last = g[..., -1]
    kg = k * jnp.exp(g_last[..., None] - g)[..., None]

    def step(state, inp):
        qk_n, u_n, w_n, qg_n, kg_n, gl_n = inp
        v_new = u_n - jnp.einsum('bhck,bhkv->bhcv', w_n, state)
        o = jnp.einsum('bhck,bhkv->bhcv', qg_n, state) + jnp.einsum('bhij,bhjv->bhiv', qk_n, v_new)
        state = state * jnp.exp(gl_n)[..., None, None] + jnp.einsum('bhck,bhcv->bhkv', kg_n, v_new)
        return state, o

    xs = (jnp.moveaxis(qk, 2, 0), jnp.moveaxis(u, 2, 0), jnp.moveaxis(w, 2, 0),
          jnp.moveaxis(qg, 2, 0), jnp.moveaxis(kg, 2, 0), jnp.moveaxis(g_last, 2, 0))
    state0 = jnp.zeros((B, H, DK, DV), jnp.float32)
    _, o = lax.scan(step, state0, xs)
    return jnp.transpose(o, (1, 0, 3, 2, 4)).reshape(B, S, H, DV)


def gated_deltanet_mixer(h, w_in, conv_w, a_log, dt_bias, norm_g, w_out):
    B, S, _ = h.shape
    f32 = jnp.float32
    proj = h @ w_in
    qkv = jax.nn.silu(causal_depthwise_conv(proj[..., :GDN_CONV_W], conv_w)).astype(f32)
    z = proj[..., GDN_CONV_W:GDN_CONV_W + GDN_V_W].astype(f32)
    b_raw = proj[..., GDN_CONV_W + GDN_V_W:GDN_CONV_W + GDN_V_W + GDN_HEADS].astype(f32)
    a_raw = proj[..., GDN_CONV_W + GDN_V_W + GDN_HEADS:].astype(f32)
    q = l2norm(qkv[..., :GDN_QK_W].reshape(B, S, GDN_HEADS, GDN_DK)) * (GDN_DK ** -0.5)
    k = l2norm(qkv[..., GDN_QK_W:2 * GDN_QK_W].reshape(B, S, GDN_HEADS, GDN_DK))
    v = qkv[..., 2 * GDN_QK_W:].reshape(B, S, GDN_HEADS, GDN_DV)
    beta = jax.nn.sigmoid(b_raw)
    g = -jnp.exp(a_log.astype(f32)) * jax.nn.softplus(a_raw + dt_bias.astype(f32))
    o = chunk_gated_delta_rule(q, k, v, g, beta)
    o = o * lax.rsqrt(jnp.mean(o * o, -1, keepdims=True) + RMS_EPS) * norm_g.astype(f32)
    o = o * jax.nn.silu(z.reshape(B, S, GDN_HEADS, GDN_DV))
    return o.reshape(B, S, GDN_V_W).astype(h.dtype) @ w_out


def shared_kv(h, kv_w, cos, sin):
    B, S, _ = h.shape
    kv = (h @ kv_w).astype(jnp.float32)
    k = kv[..., :ATT_Q_W].reshape(B, S, N_GROUPS, HEADS_PER_GROUP, HEAD_DIM)
    v = kv[..., ATT_Q_W:].reshape(B, S, N_GROUPS, HEADS_PER_GROUP, HEAD_DIM)
    return apply_rope(k, cos, sin), v


def dilated_window_attention(q, k, v, window, dilation):
    B, S, H, D = q.shape
    steps = window // dilation
    span = dilation * ATT_BLOCK
    s_pad = -(-S // span) * span
    pad = s_pad - S
    if pad:
        padw = ((0, 0), (0, pad), (0, 0), (0, 0))
        q, k, v = jnp.pad(q, padw), jnp.pad(k, padw), jnp.pad(v, padw)
    nb = s_pad // span

    def to_blocks(a):
        return a.reshape(B, nb, ATT_BLOCK, dilation, H, D).transpose(0, 3, 4, 1, 2, 5)

    def with_prev(a):
        prev = jnp.concatenate([jnp.zeros_like(a[:, :, :, :1]), a[:, :, :, :-1]], axis=3)
        return jnp.concatenate([prev, a], axis=-2)

    qb = to_blocks(q)
    kk = with_prev(to_blocks(k))
    vv = with_prev(to_blocks(v))
    i = jnp.arange(ATT_BLOCK)[:, None]
    j = jnp.arange(2 * ATT_BLOCK)[None, :]
    off = i + ATT_BLOCK - j
    band = (off >= 0) & (off <= steps)
    blk = jnp.arange(nb)[:, None, None]
    mask = band[None] & ((blk > 0) | (j >= ATT_BLOCK)[None])
    s = jnp.einsum('brhnqd,brhnkd->brhnqk', qb, kk)
    s = jnp.where(mask, s, -jnp.inf)
    m = jnp.max(s, -1, keepdims=True)
    p = jnp.exp(s - m)
    den = jnp.sum(p, -1, keepdims=True)
    o = jnp.einsum('brhnqk,brhnkd->brhnqd', p, vv) / den
    lse = (m + jnp.log(den))[..., 0]
    o = o.transpose(0, 3, 4, 1, 2, 5).reshape(B, s_pad, H, D)[:, :S]
    lse = lse.transpose(0, 3, 4, 1, 2).reshape(B, s_pad, H)[:, :S]
    return o, lse


def dilated_attention_mixer(h, k_sh, v_sh, w_in, w_out, cos, sin):
    B, S, _ = h.shape
    proj = h @ w_in
    q = proj[..., :ATT_Q_W].astype(jnp.float32).reshape(B, S, N_GROUPS, HEADS_PER_GROUP, HEAD_DIM)
    z = proj[..., ATT_Q_W:].astype(jnp.float32)
    q = apply_rope(q, cos, sin) * (HEAD_DIM ** -0.5)
    outs, lses = [], []
    for gi, (window, dilation) in enumerate(DIL_GROUPS):
        o, lse = dilated_window_attention(q[:, :, gi], k_sh[:, :, gi], v_sh[:, :, gi], window, dilation)
        outs.append(o)
        lses.append(lse)
    wts = jax.nn.softmax(jnp.stack(lses, 0), axis=0)
    o = jnp.sum(wts[..., None] * jnp.stack(outs, 0), 0).reshape(B, S, ATT_OUT_W)
    o = o * jax.nn.silu(z)
    return o.astype(h.dtype) @ w_out


def setup_inputs(seed: int = 0) -> dict:
    key = jax.random.key(seed)
    ks = jax.random.split(key, 14)
    n_a = DEPTH // 2
    n_b = DEPTH - n_a
    beta_init = (8.0 * DEPTH) ** -0.25
    f32 = jnp.float32
    x = jax.random.normal(ks[0], (BATCH, SEQ, D_MODEL), f32)
    ln_g = 1.0 + 0.02 * jax.random.normal(ks[1], (DEPTH, D_MODEL), f32)
    ln_b = 0.02 * jax.random.normal(ks[2], (DEPTH, D_MODEL), f32)
    gdn_w_in = jax.random.normal(ks[3], (n_a, D_MODEL, GDN_IN_W), f32) * D_MODEL ** -0.5
    gdn_conv_w = jax.random.normal(ks[4], (n_a, CONV_K, GDN_CONV_W), f32) * CONV_K ** -0.5
    gdn_a_log = jnp.log(jax.random.uniform(ks[5], (n_a, GDN_HEADS), f32, 1.0, 16.0))
    dt = jnp.exp(jax.random.uniform(ks[6], (n_a, GDN_HEADS), f32, math.log(1e-3), math.log(1e-1)))
    gdn_dt_bias = dt + jnp.log(-jnp.expm1(-dt))
    gdn_norm_g = 1.0 + 0.02 * jax.random.normal(ks[7], (n_a, GDN_DV), f32)
    gdn_w_out = jax.random.normal(ks[8], (n_a, GDN_V_W, D_MODEL), f32) * (GDN_V_W ** -0.5) * beta_init
    kv_w = jax.random.normal(ks[9], (D_MODEL, 2 * ATT_Q_W), f32) * D_MODEL ** -0.5
    att_w_in = jax.random.normal(ks[10], (n_b, D_MODEL, ATT_IN_W), f32) * D_MODEL ** -0.5
    att_w_out = jax.random.normal(ks[11], (n_b, ATT_OUT_W, D_MODEL), f32) * (ATT_OUT_W ** -0.5) * beta_init
    return {'x': x, 'ln_g': ln_g, 'ln_b': ln_b, 'gdn_w_in': gdn_w_in, 'gdn_conv_w': gdn_conv_w,
            'gdn_a_log': gdn_a_log, 'gdn_dt_bias': gdn_dt_bias, 'gdn_norm_g': gdn_norm_g,
            'gdn_w_out': gdn_w_out, 'kv_w': kv_w, 'att_w_in': att_w_in, 'att_w_out': att_w_out}


def reference(x, ln_g, ln_b, gdn_w_in, gdn_conv_w, gdn_a_log, gdn_dt_bias, gdn_norm_g,
              gdn_w_out, kv_w, att_w_in, att_w_out):
    n_a = DEPTH // 2
    alpha = (2.0 * DEPTH) ** 0.25
    cos, sin = rope_tables(x.shape[1])
    h = x
    k_sh = None
    v_sh = None
    for layer in range(DEPTH):
        if layer < n_a:
            y = gated_deltanet_mixer(h, gdn_w_in[layer], gdn_conv_w[layer], gdn_a_log[layer],
                                     gdn_dt_bias[layer], gdn_norm_g[layer], gdn_w_out[layer])
        else:
            if layer == n_a:
                k_sh, v_sh = shared_kv(h, kv_w, cos, sin)
            jb = layer - n_a
            y = dilated_attention_mixer(h, k_sh, v_sh, att_w_in[jb], att_w_out[jb], cos, sin)
        h = layer_norm(alpha * h + y, ln_g[layer], ln_b[layer])
    return h
```

```python
import functools
import math

import jax
import jax.numpy as jnp
from jax import lax
from jax.experimental import pallas as pl
from jax.experimental.pallas import tpu as pltpu

F32 = jnp.float32
BF16 = jnp.bfloat16

D_MODEL = 1024
DEPTH = 2
ALPHA = (2.0 * DEPTH) ** 0.25
LN_EPS = 1e-5
RMS_EPS = 1e-6

GDN_HEADS = 8
GDN_DK = 128
GDN_DV = 256
GDN_QK_W = GDN_HEADS * GDN_DK
GDN_V_W = GDN_HEADS * GDN_DV
GDN_CONV_W = 2 * GDN_QK_W + GDN_V_W
CONV_K = 4
CHUNK = 64
CONV_GROUP = 1024

DIL_GROUPS = ((128, 1), (512, 4), (2048, 16))
N_GROUPS = 3
HEADS_PER_GROUP = 4
HEAD_DIM = 128
GROUP_W = HEADS_PER_GROUP * HEAD_DIM
ATT_Q_W = N_GROUPS * GROUP_W
ATT_BLOCK = 128
ATT_SPAN = 2048
ROPE_THETA = 10000.0

SC_GC, SC_BETA, SC_EGC, SC_EGR, SC_BEGE, SC_EGL = 0, 8, 16, 24, 32, 40

V7X_VMEM_LIMIT = 56 * 1024 * 1024
NEG_BIG = -1e30


def _resident(shape):
    nd = len(shape)
    return pl.BlockSpec(shape, lambda *_: (0,) * nd, pipeline_mode=pl.Buffered(1))


def _layer_norm(r, g, b):
    mu = jnp.mean(r, axis=-1, keepdims=True)
    c = r - mu
    var = jnp.mean(c * c, axis=-1, keepdims=True)
    return c * lax.rsqrt(var + LN_EPS) * g + b


def _silu(x):
    return x * jax.nn.sigmoid(x)


def _split3(x):
    hi = x.astype(BF16)
    r1 = x - hi.astype(F32)
    mid = r1.astype(BF16)
    lo = (r1 - mid.astype(F32)).astype(BF16)
    return hi, mid, lo


def _gdn_in_kernel(x_ref, w_ref, wba_ref, cw_ref, alog_ref, dtb_ref,
                   q_ref, k_ref, v_ref, z_ref, scal_ref,
                   pc_ref, tail_ref, *, tile):
    hdr = 8
    xb = x_ref[0].astype(BF16)

    @pl.when(pl.program_id(1) == 0)
    def _():
        tail_ref[...] = jnp.zeros_like(tail_ref)

    def conv_slab(g):
        c0 = g * CONV_GROUP
        p = jnp.dot(xb, w_ref[:, c0:c0 + CONV_GROUP], preferred_element_type=F32)
        pc_ref[0:hdr, :] = tail_ref[:, c0:c0 + CONV_GROUP]
        pc_ref[hdr:hdr + tile, :] = p
        tail_ref[:, c0:c0 + CONV_GROUP] = p[tile - hdr:, :]
        y = cw_ref[CONV_K - 1:CONV_K, c0:c0 + CONV_GROUP] * p
        for j in range(CONV_K - 1):
            off = hdr - (CONV_K - 1) + j
            y = y + cw_ref[j:j + 1, c0:c0 + CONV_GROUP] * pc_ref[off:off + tile, :]
        return _silu(y)

    def l2n(y, scale):
        outs = []
        for h in range(GDN_HEADS):
            yh = y[:, h * GDN_DK:(h + 1) * GDN_DK]
            ss = jnp.sum(yh * yh, axis=-1, keepdims=True)
            outs.append(yh * (lax.rsqrt(ss + RMS_EPS) * scale))
        return outs

    for h, qh in enumerate(l2n(conv_slab(0), GDN_DK ** -0.5)):
        q_ref[0, :, h * GDN_DK:(h + 1) * GDN_DK] = qh.astype(BF16)
    for h, kh in enumerate(l2n(conv_slab(1), 1.0)):
        k_ref[0, :, h * GDN_DK:(h + 1) * GDN_DK] = kh.astype(BF16)
    for g in range(2, GDN_CONV_W // CONV_GROUP):
        c0 = (g - 2) * CONV_GROUP
        v_ref[0, :, c0:c0 + CONV_GROUP] = conv_slab(g).astype(BF16)
    for g in range(GDN_V_W // CONV_GROUP):
        c0 = GDN_CONV_W + g * CONV_GROUP
        z = jnp.dot(xb, w_ref[:, c0:c0 + CONV_GROUP], preferred_element_type=F32)
        z_ref[0, :, g * CONV_GROUP:(g + 1) * CONV_GROUP] = z.astype(BF16)

    ba = jnp.dot(xb, wba_ref[...], preferred_element_type=F32)
    beta = jax.nn.sigmoid(ba[:, 0:128])
    a_in = ba[:, 128:256] + dtb_ref[...]
    softplus = jnp.maximum(a_in, 0.0) + jnp.log1p(jnp.exp(-jnp.abs(a_in)))
    g = -jnp.exp(alog_ref[...]) * softplus

    row = lax.broadcasted_iota(jnp.int32, (tile, tile), 0)
    col = lax.broadcasted_iota(jnp.int32, (tile, tile), 1)
    shift = CHUNK.bit_length() - 1
    same = jnp.right_shift(row, shift) == jnp.right_shift(col, shift)
    l_all = jnp.where(same, 1.0, 0.0)
    l_cum = jnp.where(col <= row, l_all, 0.0).astype(BF16)
    l_all = l_all.astype(BF16)
    lmat = jnp.concatenate([l_cum, l_all], axis=0)
    acc = jnp.zeros((2 * tile, 128), F32)
    for piece in _split3(g):
        acc = acc + jnp.dot(lmat, piece, preferred_element_type=F32)
    gc = acc[:tile]
    gl = acc[tile:]
    egc = jnp.exp(gc)
    egr = jnp.exp(gl - gc)
    egl = jnp.exp(gl)
    bege = beta * egc

    lane = lax.broadcasted_iota(jnp.int32, (tile, 128), 1)
    packed = jnp.zeros((tile, 128), F32)
    for slot, val in ((SC_GC, gc), (SC_BETA, beta), (SC_EGC, egc), (SC_EGR, egr),
                      (SC_BEGE, bege), (SC_EGL, egl)):
        shifted = val if slot == 0 else pltpu.roll(val, slot, axis=1)
        packed = jnp.where((lane >= slot) & (lane < slot + GDN_HEADS), shifted, packed)
    scal_ref[0] = packed


def _gdn_in(x, w_main, w_ba, conv_w, alog, dtb, *, tile):
    B, S, D = x.shape
    grid = (B, S // tile)
    tok = lambda w: pl.BlockSpec((1, tile, w), lambda b, s: (b, s, 0))
    return pl.pallas_call(
        functools.partial(_gdn_in_kernel, tile=tile),
        grid=grid,
        in_specs=[tok(D), _resident(w_main.shape), _resident(w_ba.shape),
                  _resident(conv_w.shape), _resident(alog.shape), _resident(dtb.shape)],
        out_specs=[tok(GDN_QK_W), tok(GDN_QK_W), tok(GDN_V_W), tok(GDN_V_W), tok(128)],
        out_shape=[jax.ShapeDtypeStruct((B, S, GDN_QK_W), BF16),
                   jax.ShapeDtypeStruct((B, S, GDN_QK_W), BF16),
                   jax.ShapeDtypeStruct((B, S, GDN_V_W), BF16),
                   jax.ShapeDtypeStruct((B, S, GDN_V_W), BF16),
                   jax.ShapeDtypeStruct((B, S, 128), F32)],
        scratch_shapes=[pltpu.VMEM((tile + 8, CONV_GROUP), F32),
                        pltpu.VMEM((8, GDN_CONV_W), F32)],
        compiler_params=pltpu.CompilerParams(
            dimension_semantics=("arbitrary", "arbitrary"),
            vmem_limit_bytes=V7X_VMEM_LIMIT),
        name="gdn_in",
    )(x, w_main, w_ba, conv_w, alog, dtb)


def _gdn_core_kernel(q_ref, k_ref, v_ref, z_ref, scal_ref, x_ref, wout_ref,
                     ng_ref, lng_ref, lnb_ref, out_ref, state_ref, o_ref, *, tile):
    C = CHUNK

    @pl.when(pl.program_id(1) == 0)
    def _():
        state_ref[...] = jnp.zeros_like(state_ref)

    ri = lax.broadcasted_iota(jnp.int32, (C, 2 * C), 0)
    li = lax.broadcasted_iota(jnp.int32, (C, 2 * C), 1)
    hi_half = li >= C
    incl = hi_half & (ri >= li - C)
    strict = hi_half & (ri > li - C)
    eye_lo = jnp.where(li == ri, 1.0, 0.0).astype(F32)
    zeros_k = jnp.zeros((C, GDN_DK), BF16)
    zeros_w = jnp.zeros((C, 2 * C), BF16)
    zeros_v = jnp.zeros((C, GDN_DV), BF16)

    def chunk_body(c, carry):
        r0 = pl.multiple_of(c * C, C)
        sc = scal_ref[0, pl.ds(r0, C), :]
        sct = jnp.concatenate([sc, sc], axis=0).T
        for h in range(GDN_HEADS):
            qh = q_ref[0, pl.ds(r0, C), h * GDN_DK:(h + 1) * GDN_DK]
            kh = k_ref[0, pl.ds(r0, C), h * GDN_DK:(h + 1) * GDN_DK]
            vh = v_ref[0, pl.ds(r0, C), h * GDN_DV:(h + 1) * GDN_DV]
            gc_c = sc[:, SC_GC + h:SC_GC + h + 1]
            beta_c = sc[:, SC_BETA + h:SC_BETA + h + 1]
            egc_c = sc[:, SC_EGC + h:SC_EGC + h + 1]
            egr_c = sc[:, SC_EGR + h:SC_EGR + h + 1]
            egl_c = sc[:, SC_EGL + h:SC_EGL + h + 1]
            gc_r = sct[SC_GC + h:SC_GC + h + 1, :]
            beta_r = sct[SC_BETA + h:SC_BETA + h + 1, :]
            bege_r = sct[SC_BEGE + h:SC_BEGE + h + 1, :]

            qk2 = lax.dot_general(jnp.concatenate([qh, kh], axis=0),
                                  jnp.concatenate([zeros_k, kh], axis=0),
                                  (((1,), (1,)), ((), ())), preferred_element_type=F32)
            decay = jnp.exp(jnp.where(incl, gc_c - gc_r, 0.0))
            qkm = jnp.where(incl, qk2[:C] * decay, 0.0)
            xm = jnp.where(strict, -(qk2[C:] * decay * beta_c), 0.0)
            w = xm + eye_lo
            for _ in range(6):
                wb = w.astype(BF16)
                y = jnp.dot(wb, jnp.concatenate([zeros_w, wb], axis=0), preferred_element_type=F32)
                w = y + jnp.where(hi_half, 0.0, w)
            tb = (w * beta_r).astype(BF16)[:, :C]
            tbg = (w * bege_r).astype(BF16)[:, :C]
            u = jnp.dot(tb, vh, preferred_element_type=F32)
            wk = jnp.dot(tbg, kh, preferred_element_type=F32)
            s_old = state_ref[h]
            r = jnp.dot(jnp.concatenate([wk.astype(BF16), qh], axis=0), s_old.astype(BF16),
                        preferred_element_type=F32)
            v_new = u - r[:C]
            vnb = v_new.astype(BF16)
            o = r[C:] * egc_c + jnp.dot(qkm.astype(BF16), jnp.concatenate([zeros_v, vnb], axis=0),
                                        preferred_element_type=F32)
            kg = (kh.astype(F32) * egr_c).astype(BF16)
            ds = lax.dot_general(kg, vnb, (((0,), (0,)), ((), ())), preferred_element_type=F32)
            egl_s = jnp.concatenate([egl_c, egl_c], axis=0)
            state_ref[h] = s_old * egl_s + ds
            o_ref[pl.ds(r0, C), h * GDN_DV:(h + 1) * GDN_DV] = o
        return carry

    lax.fori_loop(0, tile // C, chunk_body, 0)

    gated = []
    for h in range(GDN_HEADS):
        oh = o_ref[:, h * GDN_DV:(h + 1) * GDN_DV]
        ms = jnp.mean(oh * oh, axis=-1, keepdims=True)
        on = oh * lax.rsqrt(ms + RMS_EPS) * ng_ref[...]
        zh = z_ref[0, :, h * GDN_DV:(h + 1) * GDN_DV].astype(F32)
        gated.append((on * _silu(zh)).astype(BF16))
    y = jnp.dot(jnp.concatenate(gated, axis=1), wout_ref[...], preferred_element_type=F32)
    out_ref[0] = _layer_norm(ALPHA * x_ref[0] + y, lng_ref[...], lnb_ref[...])


def _gdn_core(q, k, v, z, scal, x, w_out, norm_g, ln_g, ln_b, *, tile):
    B, S, D = x.shape
    grid = (B, S // tile)
    tok = lambda w: pl.BlockSpec((1, tile, w), lambda b, s: (b, s, 0))
    return pl.pallas_call(
        functools.partial(_gdn_core_kernel, tile=tile),
        grid=grid,
        in_specs=[tok(GDN_QK_W), tok(GDN_QK_W), tok(GDN_V_W), tok(GDN_V_W), tok(128), tok(D),
                  _resident(w_out.shape), _resident(norm_g.shape), _resident(ln_g.shape),
                  _resident(ln_b.shape)],
        out_specs=tok(D),
        out_shape=jax.ShapeDtypeStruct((B, S, D), F32),
        scratch_shapes=[pltpu.VMEM((GDN_HEADS, GDN_DK, GDN_DV), F32),
                        pltpu.VMEM((tile, GDN_V_W), F32)],
        compiler_params=pltpu.CompilerParams(
            dimension_semantics=("arbitrary", "arbitrary"),
            vmem_limit_bytes=V7X_VMEM_LIMIT),
        name="gdn_core",
    )(q, k, v, z, scal, x, w_out, norm_g, ln_g, ln_b)


def _att_in_kernel(h_ref, w_ref, cos_ref, sin_ref, *out_refs):
    q_refs, k_refs, v_refs, z_ref = out_refs[0:3], out_refs[3:6], out_refs[6:9], out_refs[9]
    hb = h_ref[0].astype(BF16)
    cos = cos_ref[...]
    sin_s = sin_ref[...]

    def rope(t, scale):
        outs = []
        for h in range(HEADS_PER_GROUP):
            th = t[:, h * HEAD_DIM:(h + 1) * HEAD_DIM]
            r = th * cos + pltpu.roll(th, HEAD_DIM // 2, axis=1) * sin_s
            outs.append(r * scale if scale != 1.0 else r)
        return outs

    for g in range(N_GROUPS):
        c0 = g * GROUP_W
        qg = jnp.dot(hb, w_ref[:, c0:c0 + GROUP_W], preferred_element_type=F32)
        for h, t in enumerate(rope(qg, HEAD_DIM ** -0.5)):
            q_refs[g][0, :, h * HEAD_DIM:(h + 1) * HEAD_DIM] = t.astype(BF16)
        c0 = ATT_Q_W + g * GROUP_W
        kg = jnp.dot(hb, w_ref[:, c0:c0 + GROUP_W], preferred_element_type=F32)
        for h, t in enumerate(rope(kg, 1.0)):
            k_refs[g][0, :, h * HEAD_DIM:(h + 1) * HEAD_DIM] = t.astype(BF16)
        c0 = 2 * ATT_Q_W + g * GROUP_W
        v_refs[g][0] = jnp.dot(hb, w_ref[:, c0:c0 + GROUP_W], preferred_element_type=F32).astype(BF16)
    c0 = 3 * ATT_Q_W
    z_ref[0] = jnp.dot(hb, w_ref[:, c0:c0 + GROUP_W], preferred_element_type=F32).astype(BF16)


def _att_in(h, w_att, cos, sin_s, *, tile):
    B, S, D = h.shape
    grid = (B, S // tile)
    tok = lambda w: pl.BlockSpec((1, tile, w), lambda b, s: (b, s, 0))
    tab = pl.BlockSpec((tile, HEAD_DIM), lambda b, s: (s, 0))
    n_out = 3 * N_GROUPS + 1
    return pl.pallas_call(
        _att_in_kernel,
        grid=grid,
        in_specs=[tok(D), _resident(w_att.shape), tab, tab],
        out_specs=[tok(GROUP_W)] * n_out,
        out_shape=[jax.ShapeDtypeStruct((B, S, GROUP_W), BF16)] * n_out,
        compiler_params=pltpu.CompilerParams(
            dimension_semantics=("arbitrary", "arbitrary"),
            vmem_limit_bytes=V7X_VMEM_LIMIT),
        name="att_in",
    )(h, w_att, cos, sin_s)


def _dil_att_kernel(q_ref, k_ref, v_ref, o_ref, lse_ref, kbuf, vbuf, *, dilation, steps):
    blk = ATT_BLOCK
    m = ATT_SPAN // (blk * dilation)
    rows = m * blk
    n = pl.program_id(1)

    @pl.when(n == 0)
    def _():
        kbuf[0:blk, :] = jnp.zeros((blk, kbuf.shape[1]), kbuf.dtype)
        vbuf[0:blk, :] = jnp.zeros((blk, vbuf.shape[1]), vbuf.dtype)

    kbuf[blk:blk + rows, :] = k_ref[0]
    vbuf[blk:blk + rows, :] = v_ref[0]

    qi = lax.broadcasted_iota(jnp.int32, (blk, 2 * blk), 0)
    kj = lax.broadcasted_iota(jnp.int32, (blk, 2 * blk), 1)
    off = qi + blk - kj
    band = (off >= 0) & (off <= steps)
    first_lo = jnp.where(n > 0, 0, blk)
    band_first = band & (kj >= first_lo)
    lane = lax.broadcasted_iota(jnp.int32, (blk, HEAD_DIM), 1)

    for i in range(m):
        mask = band_first if i == 0 else band
        for r in range(dilation):
            lse_tile = jnp.zeros((blk, HEAD_DIM), F32)
            for h in range(HEADS_PER_GROUP):
                c0 = r * GROUP_W + h * HEAD_DIM
                qt = q_ref[0, i * blk:(i + 1) * blk, c0:c0 + HEAD_DIM]
                kt = kbuf[i * blk:(i + 2) * blk, c0:c0 + HEAD_DIM]
                vt = vbuf[i * blk:(i + 2) * blk, c0:c0 + HEAD_DIM]
                s = lax.dot_general(qt, kt, (((1,), (1,)), ((), ())), preferred_element_type=F32)
                s = jnp.where(mask, s, NEG_BIG)
                mx = jnp.max(s, axis=-1, keepdims=True)
                p = jnp.exp(s - mx)
                den = jnp.sum(p, axis=-1, keepdims=True)
                o = jnp.dot(p.astype(BF16), vt, preferred_element_type=F32) / den
                o_ref[0, i * blk:(i + 1) * blk, c0:c0 + HEAD_DIM] = o.astype(BF16)
                lse_tile = jnp.where(lane == h, mx + jnp.log(den), lse_tile)
            lse_ref[0, i * blk:(i + 1) * blk, r * HEAD_DIM:(r + 1) * HEAD_DIM] = lse_tile

    kbuf[0:blk, :] = kbuf[rows:rows + blk, :]
    vbuf[0:blk, :] = vbuf[rows:rows + blk, :]


def _dil_att(q, k, v, *, window, dilation):
    B, S, _ = q.shape
    d = dilation
    rows = ATT_SPAN // d
    wide = d * GROUP_W
    view = lambda a: a.reshape(B, S // d, wide)
    grid = (B, S // ATT_SPAN)
    spec = pl.BlockSpec((1, rows, wide), lambda b, n: (b, n, 0))
    lse_spec = pl.BlockSpec((1, rows, d * HEAD_DIM), lambda b, n: (b, n, 0))
    o, lse = pl.pallas_call(
        functools.partial(_dil_att_kernel, dilation=d, steps=window // d),
        grid=grid,
        in_specs=[spec, spec, spec],
        out_specs=[spec, lse_spec],
        out_shape=[jax.ShapeDtypeStruct((B, S // d, wide), BF16),
                   jax.ShapeDtypeStruct((B, S // d, d * HEAD_DIM), F32)],
        scratch_shapes=[pltpu.VMEM((rows + ATT_BLOCK, wide), BF16),
                        pltpu.VMEM((rows + ATT_BLOCK, wide), BF16)],
        compiler_params=pltpu.CompilerParams(
            dimension_semantics=("arbitrary", "arbitrary"),
            vmem_limit_bytes=V7X_VMEM_LIMIT),
        name=f"dil_att_d{d}",
    )(view(q), view(k), view(v))
    return o.reshape(B, S, GROUP_W), lse.reshape(B, S, HEAD_DIM)


def _att_out_kernel(o0_ref, o1_ref, o2_ref, l0_ref, l1_ref, l2_ref, z_ref, h_ref,
                    wout_ref, lng_ref, lnb_ref, out_ref):
    lses = [l0_ref[0], l1_ref[0], l2_ref[0]]
    mx = jnp.maximum(jnp.maximum(lses[0], lses[1]), lses[2])
    es = [jnp.exp(l - mx) for l in lses]
    inv = 1.0 / (es[0] + es[1] + es[2])
    wts = [e * inv for e in es]
    o_refs = [o0_ref, o1_ref, o2_ref]
    merged = []
    for h in range(HEADS_PER_GROUP):
        sl = slice(h * HEAD_DIM, (h + 1) * HEAD_DIM)
        acc = None
        for g in range(N_GROUPS):
            t = wts[g][:, h:h + 1] * o_refs[g][0, :, sl].astype(F32)
            acc = t if acc is None else acc + t
        zh = z_ref[0, :, sl].astype(F32)
        merged.append((acc * _silu(zh)).astype(BF16))
    y = jnp.dot(jnp.concatenate(merged, axis=1), wout_ref[...], preferred_element_type=F32)
    out_ref[0] = _layer_norm(ALPHA * h_ref[0] + y, lng_ref[...], lnb_ref[...])


def _att_out(os_, lses, z, h, w_out, ln_g, ln_b, *, tile):
    B, S, D = h.shape
    grid = (B, S // tile)
    tok = lambda w: pl.BlockSpec((1, tile, w), lambda b, s: (b, s, 0))
    return pl.pallas_call(
        _att_out_kernel,
        grid=grid,
        in_specs=[tok(GROUP_W)] * 3 + [tok(HEAD_DIM)] * 3 + [tok(GROUP_W), tok(D),
                  _resident(w_out.shape), _resident(ln_g.shape), _resident(ln_b.shape)],
        out_specs=tok(D),
        out_shape=jax.ShapeDtypeStruct((B, S, D), F32),
        compiler_params=pltpu.CompilerParams(
            dimension_semantics=("arbitrary", "arbitrary"),
            vmem_limit_bytes=V7X_VMEM_LIMIT),
        name="att_out",
    )(*os_, *lses, z, h, w_out, ln_g, ln_b)


def _rope_tables(seq_len):
    inv = 1.0 / (ROPE_THETA ** (jnp.arange(0, HEAD_DIM, 2, dtype=F32) / HEAD_DIM))
    ang = jnp.arange(seq_len, dtype=F32)[:, None] * inv[None, :]
    ang = jnp.concatenate([ang, ang], -1)
    sign = jnp.where(jnp.arange(HEAD_DIM) < HEAD_DIM // 2, -1.0, 1.0).astype(F32)
    return jnp.cos(ang), jnp.sin(ang) * sign[None, :]


def _pad_lanes(v, width=128):
    return jnp.zeros((1, width), F32).at[0, :v.shape[0]].set(v.astype(F32))


def kernel(x, ln_g, ln_b, gdn_w_in, gdn_conv_w, gdn_a_log, gdn_dt_bias, gdn_norm_g, gdn_w_out,
           kv_w, att_w_in, att_w_out):
    B, S, D = x.shape
    w_in = gdn_w_in[0]
    w_main = w_in[:, :GDN_CONV_W + GDN_V_W].astype(BF16)
    n_main = GDN_CONV_W + GDN_V_W
    w_ba = jnp.zeros((D, 256), F32)
    w_ba = w_ba.at[:, 0:GDN_HEADS].set(w_in[:, n_main:n_main + GDN_HEADS])
    w_ba = w_ba.at[:, 128:128 + GDN_HEADS].set(w_in[:, n_main + GDN_HEADS:]).astype(BF16)
    q, k, v, z, scal = _gdn_in(x, w_main, w_ba, gdn_conv_w[0], _pad_lanes(gdn_a_log[0]),
                               _pad_lanes(gdn_dt_bias[0]), tile=256)
    h1 = _gdn_core(q, k, v, z, scal, x, gdn_w_out[0].astype(BF16), gdn_norm_g[0][None, :],
                   ln_g[0][None, :], ln_b[0][None, :], tile=256)
    w_att = jnp.concatenate([att_w_in[0][:, :ATT_Q_W], kv_w, att_w_in[0][:, ATT_Q_W:]], axis=1).astype(BF16)
    cos, sin_s = _rope_tables(S)
    outs = _att_in(h1, w_att, cos, sin_s, tile=256)
    qs, ks, vs, za = outs[0:3], outs[3:6], outs[6:9], outs[9]
    os_, lses = [], []
    for g, (window, dilation) in enumerate(DIL_GROUPS):
        o, lse = _dil_att(qs[g], ks[g], vs[g], window=window, dilation=dilation)
        os_.append(o)
        lses.append(lse)
    return _att_out(os_, lses, za, h1, att_w_out[0].astype(BF16), ln_g[1][None, :], ln_b[1][None, :],
                    tile=512)
```

```python
import functools
import math

import jax
import jax.numpy as jnp
from jax import lax
from jax.experimental import pallas as pl
from jax.experimental.pallas import tpu as pltpu

F32 = jnp.float32
BF16 = jnp.bfloat16

D_MODEL = 1024
DEPTH = 2
ALPHA = (2.0 * DEPTH) ** 0.25
LN_EPS = 1e-5
RMS_EPS = 1e-6

GDN_HEADS = 8
GDN_DK = 128
GDN_DV = 256
GDN_QK_W = GDN_HEADS * GDN_DK
GDN_V_W = GDN_HEADS * GDN_DV
GDN_CONV_W = 2 * GDN_QK_W + GDN_V_W
CONV_K = 4
CHUNK = 64
CONV_GROUP = 1024

DIL_GROUPS = ((128, 1), (512, 4), (2048, 16))
N_GROUPS = 3
HEADS_PER_GROUP = 4
HEAD_DIM = 128
GROUP_W = HEADS_PER_GROUP * HEAD_DIM
ATT_Q_W = N_GROUPS * GROUP_W
ATT_BLOCK = 128
ATT_SPAN = 2048
ROPE_THETA = 10000.0

SC_GC, SC_BETA, SC_EGC, SC_EGR, SC_BEGE, SC_EGL = 0, 8, 16, 24, 32, 40

V7X_VMEM_LIMIT = 56 * 1024 * 1024
NEG_BIG = -1e30


def _resident(shape):
    nd = len(shape)
    return pl.BlockSpec(shape, lambda *_: (0,) * nd, pipeline_mode=pl.Buffered(1))


def _layer_norm(r, g, b):
    mu = jnp.mean(r, axis=-1, keepdims=True)
    c = r - mu
    var = jnp.mean(c * c, axis=-1, keepdims=True)
    return c * lax.rsqrt(var + LN_EPS) * g + b


def _silu(x):
    return x * jax.nn.sigmoid(x)


def _split3(x):
    hi = x.astype(BF16)
    r1 = x - hi.astype(F32)
    mid = r1.astype(BF16)
    lo = (r1 - mid.astype(F32)).astype(BF16)
    return hi, mid, lo


def _gdn_in_kernel(x_ref, w_ref, wba_ref, cw_ref, alog_ref, dtb_ref,
                   q_ref, k_ref, v_ref, z_ref, scal_ref,
                   pc_ref, tail_ref, *, tile):
    hdr = 8
    xb = x_ref[0].astype(BF16)

    @pl.when(pl.program_id(1) == 0)
    def _():
        tail_ref[...] = jnp.zeros_like(tail_ref)

    def conv_slab(g):
        c0 = g * CONV_GROUP
        p = jnp.dot(xb, w_ref[:, c0:c0 + CONV_GROUP], preferred_element_type=F32)
        pc_ref[0:hdr, :] = tail_ref[:, c0:c0 + CONV_GROUP]
        pc_ref[hdr:hdr + tile, :] = p
        tail_ref[:, c0:c0 + CONV_GROUP] = p[tile - hdr:, :]
        y = cw_ref[CONV_K - 1:CONV_K, c0:c0 + CONV_GROUP] * p
        for j in range(CONV_K - 1):
            off = hdr - (CONV_K - 1) + j
            y = y + cw_ref[j:j + 1, c0:c0 + CONV_GROUP] * pc_ref[off:off + tile, :]
        return _silu(y)

    def l2n(y, scale):
        outs = []
        for h in range(GDN_HEADS):
            yh = y[:, h * GDN_DK:(h + 1) * GDN_DK]
            ss = jnp.sum(yh * yh, axis=-1, keepdims=True)
            outs.append(yh * (lax.rsqrt(ss + RMS_EPS) * scale))
        return outs

    for h, qh in enumerate(l2n(conv_slab(0), GDN_DK ** -0.5)):
        q_ref[0, :, h * GDN_DK:(h + 1) * GDN_DK] = qh.astype(BF16)
    for h, kh in enumerate(l2n(conv_slab(1), 1.0)):
        k_ref[0, :, h * GDN_DK:(h + 1) * GDN_DK] = kh.astype(BF16)
    for g in range(2, GDN_CONV_W // CONV_GROUP):
        c0 = (g - 2) * CONV_GROUP
        v_ref[0, :, c0:c0 + CONV_GROUP] = conv_slab(g).astype(BF16)
    for g in range(GDN_V_W // CONV_GROUP):
        c0 = GDN_CONV_W + g * CONV_GROUP
        z = jnp.dot(xb, w_ref[:, c0:c0 + CONV_GROUP], preferred_element_type=F32)
        z_ref[0, :, g * CONV_GROUP:(g + 1) * CONV_GROUP] = z.astype(BF16)

    ba = jnp.dot(xb, wba_ref[...], preferred_element_type=F32)
    beta = jax.nn.sigmoid(ba[:, 0:128])
    a_in = ba[:, 128:256] + dtb_ref[...]
    softplus = jnp.maximum(a_in, 0.0) + jnp.log1p(jnp.exp(-jnp.abs(a_in)))
    g = -jnp.exp(alog_ref[...]) * softplus

    row = lax.broadcasted_iota(jnp.int32, (tile, tile), 0)
    col = lax.broadcasted_iota(jnp.int32, (tile, tile), 1)
    shift = CHUNK.bit_length() - 1
    same = jnp.right_shift(row, shift) == jnp.right_shift(col, shift)
    l_all = jnp.where(same, 1.0, 0.0)
    l_cum = jnp.where(col <= row, l_all, 0.0).astype(BF16)
    l_all = l_all.astype(BF16)
    lmat = jnp.concatenate([l_cum, l_all], axis=0)
    acc = jnp.zeros((2 * tile, 128), F32)
    for piece in _split3(g):
        acc = acc + jnp.dot(lmat, piece, preferred_element_type=F32)
    gc = acc[:tile]
    gl = acc[tile:]
    egc = jnp.exp(gc)
    egr = jnp.exp(gl - gc)
    egl = jnp.exp(gl)
    bege = beta * egc

    lane = lax.broadcasted_iota(jnp.int32, (tile, 128), 1)
    packed = jnp.zeros((tile, 128), F32)
    for slot, val in ((SC_GC, gc), (SC_BETA, beta), (SC_EGC, egc), (SC_EGR, egr),
                      (SC_BEGE, bege), (SC_EGL, egl)):
        shifted = val if slot == 0 else pltpu.roll(val, slot, axis=1)
        packed = jnp.where((lane >= slot) & (lane < slot + GDN_HEADS), shifted, packed)
    scal_ref[0] = packed


def _gdn_in(x, w_main, w_ba, conv_w, alog, dtb, *, tile):
    B, S, D = x.shape
    grid = (B, S // tile)
    tok = lambda w: pl.BlockSpec((1, tile, w), lambda b, s: (b, s, 0))
    return pl.pallas_call(
        functools.partial(_gdn_in_kernel, tile=tile),
        grid=grid,
        in_specs=[tok(D), _resident(w_main.shape), _resident(w_ba.shape),
                  _resident(conv_w.shape), _resident(alog.shape), _resident(dtb.shape)],
        out_specs=[tok(GDN_QK_W), tok(GDN_QK_W), tok(GDN_V_W), tok(GDN_V_W), tok(128)],
        out_shape=[jax.ShapeDtypeStruct((B, S, GDN_QK_W), BF16),
                   jax.ShapeDtypeStruct((B, S, GDN_QK_W), BF16),
                   jax.ShapeDtypeStruct((B, S, GDN_V_W), BF16),
                   jax.ShapeDtypeStruct((B, S, GDN_V_W), BF16),
                   jax.ShapeDtypeStruct((B, S, 128), F32)],
        scratch_shapes=[pltpu.VMEM((tile + 8, CONV_GROUP), F32),
                        pltpu.VMEM((8, GDN_CONV_W), F32)],
        compiler_params=pltpu.CompilerParams(
            dimension_semantics=("arbitrary", "arbitrary"),
            vmem_limit_bytes=V7X_VMEM_LIMIT),
        name="gdn_in",
    )(x, w_main, w_ba, conv_w, alog, dtb)


def _gdn_core_kernel(q_ref, k_ref, v_ref, z_ref, scal_ref, x_ref, wout_ref,
                     ng_ref, lng_ref, lnb_ref, out_ref, state_ref, o_ref, *, tile):
    C = CHUNK

    @pl.when(pl.program_id(1) == 0)
    def _():
        state_ref[...] = jnp.zeros_like(state_ref)

    ri = lax.broadcasted_iota(jnp.int32, (C, 2 * C), 0)
    li = lax.broadcasted_iota(jnp.int32, (C, 2 * C), 1)
    hi_half = li >= C
    incl = hi_half & (ri >= li - C)
    strict = hi_half & (ri > li - C)
    eye_lo = jnp.where(li == ri, 1.0, 0.0).astype(F32)
    zeros_k = jnp.zeros((C, GDN_DK), BF16)
    zeros_w = jnp.zeros((C, 2 * C), BF16)
    zeros_v = jnp.zeros((C, GDN_DV), BF16)

    heads = range(GDN_HEADS)

    def chunk_body(c, carry):
        r0 = pl.multiple_of(c * C, C)
        sc = scal_ref[0, pl.ds(r0, C), :]
        sct = jnp.concatenate([sc, sc], axis=0).T
        col = lambda slot, h: sc[:, slot + h:slot + h + 1]
        row = lambda slot, h: sct[slot + h:slot + h + 1, :]
        qs = [q_ref[0, pl.ds(r0, C), h * GDN_DK:(h + 1) * GDN_DK] for h in heads]
        ks = [k_ref[0, pl.ds(r0, C), h * GDN_DK:(h + 1) * GDN_DK] for h in heads]
        vs = [v_ref[0, pl.ds(r0, C), h * GDN_DV:(h + 1) * GDN_DV] for h in heads]

        qk2 = [lax.dot_general(jnp.concatenate([qs[h], ks[h]], axis=0),
                               jnp.concatenate([zeros_k, ks[h]], axis=0),
                               (((1,), (1,)), ((), ())), preferred_element_type=F32) for h in heads]
        decay = [jnp.exp(jnp.where(incl, col(SC_GC, h) - row(SC_GC, h), 0.0)) for h in heads]
        qkm = [jnp.where(incl, qk2[h][:C] * decay[h], 0.0).astype(BF16) for h in heads]
        w = [jnp.where(strict, -(qk2[h][C:] * decay[h] * col(SC_BETA, h)), 0.0) + eye_lo for h in heads]
        for _ in range(6):
            wb = [w[h].astype(BF16) for h in heads]
            y = [jnp.dot(wb[h], jnp.concatenate([zeros_w, wb[h]], axis=0), preferred_element_type=F32)
                 for h in heads]
            w = [y[h] + jnp.where(hi_half, 0.0, w[h]) for h in heads]
        tb = [(w[h] * row(SC_BETA, h)).astype(BF16)[:, :C] for h in heads]
        tbg = [(w[h] * row(SC_BEGE, h)).astype(BF16)[:, :C] for h in heads]
        u = [jnp.dot(tb[h], vs[h], preferred_element_type=F32) for h in heads]
        wk = [jnp.dot(tbg[h], ks[h], preferred_element_type=F32) for h in heads]
        kg = [(ks[h].astype(F32) * col(SC_EGR, h)).astype(BF16) for h in heads]
        s_old = [state_ref[h] for h in heads]
        r = [jnp.dot(jnp.concatenate([wk[h].astype(BF16), qs[h]], axis=0), s_old[h].astype(BF16),
                     preferred_element_type=F32) for h in heads]
        vnb = [(u[h] - r[h][:C]).astype(BF16) for h in heads]
        o2 = [jnp.dot(qkm[h], jnp.concatenate([zeros_v, vnb[h]], axis=0), preferred_element_type=F32)
              for h in heads]
        ds = [lax.dot_general(kg[h], vnb[h], (((0,), (0,)), ((), ())), preferred_element_type=F32)
              for h in heads]
        for h in heads:
            egl_c = col(SC_EGL, h)
            state_ref[h] = s_old[h] * jnp.concatenate([egl_c, egl_c], axis=0) + ds[h]
            o_ref[pl.ds(r0, C), h * GDN_DV:(h + 1) * GDN_DV] = r[h][C:] * col(SC_EGC, h) + o2[h]
        return carry

    lax.fori_loop(0, tile // C, chunk_body, 0)

    gated = []
    for h in range(GDN_HEADS):
        oh = o_ref[:, h * GDN_DV:(h + 1) * GDN_DV]
        ms = jnp.mean(oh * oh, axis=-1, keepdims=True)
        on = oh * lax.rsqrt(ms + RMS_EPS) * ng_ref[...]
        zh = z_ref[0, :, h * GDN_DV:(h + 1) * GDN_DV].astype(F32)
        gated.append((on * _silu(zh)).astype(BF16))
    y = jnp.dot(jnp.concatenate(gated, axis=1), wout_ref[...], preferred_element_type=F32)
    out_ref[0] = _layer_norm(ALPHA * x_ref[0] + y, lng_ref[...], lnb_ref[...])


def _gdn_core(q, k, v, z, scal, x, w_out, norm_g, ln_g, ln_b, *, tile):
    B, S, D = x.shape
    grid = (B, S // tile)
    tok = lambda w: pl.BlockSpec((1, tile, w), lambda b, s: (b, s, 0))
    return pl.pallas_call(
        functools.partial(_gdn_core_kernel, tile=tile),
        grid=grid,
        in_specs=[tok(GDN_QK_W), tok(GDN_QK_W), tok(GDN_V_W), tok(GDN_V_W), tok(128), tok(D),
                  _resident(w_out.shape), _resident(norm_g.shape), _resident(ln_g.shape),
                  _resident(ln_b.shape)],
        out_specs=tok(D),
        out_shape=jax.ShapeDtypeStruct((B, S, D), F32),
        scratch_shapes=[pltpu.VMEM((GDN_HEADS, GDN_DK, GDN_DV), F32),
                        pltpu.VMEM((tile, GDN_V_W), F32)],
        compiler_params=pltpu.CompilerParams(
            dimension_semantics=("arbitrary", "arbitrary"),
            vmem_limit_bytes=V7X_VMEM_LIMIT),
        name="gdn_core",
    )(q, k, v, z, scal, x, w_out, norm_g, ln_g, ln_b)


def _att_in_kernel(h_ref, w_ref, cos_ref, sin_ref, *refs, tile):
    out_refs, stage_ref = refs[:-1], refs[-1]
    q_refs, k_refs, v_refs, z_ref = out_refs[0:3], out_refs[3:6], out_refs[6:9], out_refs[9]
    hb = h_ref[0].astype(BF16)
    cos = cos_ref[...]
    sin_s = sin_ref[...]

    def rope(t, scale):
        outs = []
        for h in range(HEADS_PER_GROUP):
            th = t[:, h * HEAD_DIM:(h + 1) * HEAD_DIM]
            r = th * cos + pltpu.roll(th, HEAD_DIM // 2, axis=1) * sin_s
            outs.append(r * scale if scale != 1.0 else r)
        return jnp.concatenate(outs, axis=1)

    def emit(out_ref, val, d):
        if d == 1:
            out_ref[0] = val.astype(BF16)
            return
        for h in range(HEADS_PER_GROUP):
            stage_ref[h] = val[:, h * HEAD_DIM:(h + 1) * HEAD_DIM]
        for r in range(d):
            for h in range(HEADS_PER_GROUP):
                c0 = r * GROUP_W + h * HEAD_DIM
                out_ref[0, :, c0:c0 + HEAD_DIM] = (
                    stage_ref[h, pl.ds(r, tile // d, stride=d), :].astype(BF16))

    for g, (_, d) in enumerate(DIL_GROUPS):
        c0 = g * GROUP_W
        qg = jnp.dot(hb, w_ref[:, c0:c0 + GROUP_W], preferred_element_type=F32)
        emit(q_refs[g], rope(qg, HEAD_DIM ** -0.5), d)
        c0 = ATT_Q_W + g * GROUP_W
        kg = jnp.dot(hb, w_ref[:, c0:c0 + GROUP_W], preferred_element_type=F32)
        emit(k_refs[g], rope(kg, 1.0), d)
        c0 = 2 * ATT_Q_W + g * GROUP_W
        emit(v_refs[g], jnp.dot(hb, w_ref[:, c0:c0 + GROUP_W], preferred_element_type=F32), d)
    c0 = 3 * ATT_Q_W
    z_ref[0] = jnp.dot(hb, w_ref[:, c0:c0 + GROUP_W], preferred_element_type=F32).astype(BF16)


def _att_in(h, w_att, cos, sin_s, *, tile):
    B, S, D = h.shape
    grid = (B, S // tile)
    tok = lambda w: pl.BlockSpec((1, tile, w), lambda b, s: (b, s, 0))
    tab = pl.BlockSpec((tile, HEAD_DIM), lambda b, s: (s, 0))
    dil = lambda d: pl.BlockSpec((1, tile // d, d * GROUP_W), lambda b, s: (b, s, 0))
    dil_shape = lambda d: jax.ShapeDtypeStruct((B, S // d, d * GROUP_W), BF16)
    ds_ = [d for _, d in DIL_GROUPS]
    return pl.pallas_call(
        functools.partial(_att_in_kernel, tile=tile),
        grid=grid,
        in_specs=[tok(D), _resident(w_att.shape), tab, tab],
        out_specs=[dil(d) for d in ds_] * 3 + [tok(GROUP_W)],
        out_shape=[dil_shape(d) for d in ds_] * 3 + [jax.ShapeDtypeStruct((B, S, GROUP_W), BF16)],
        scratch_shapes=[pltpu.VMEM((HEADS_PER_GROUP, tile, HEAD_DIM), F32)],
        compiler_params=pltpu.CompilerParams(
            dimension_semantics=("arbitrary", "arbitrary"),
            vmem_limit_bytes=V7X_VMEM_LIMIT),
        name="att_in",
    )(h, w_att, cos, sin_s)


def _dil_att_kernel(q_ref, k_ref, v_ref, o_ref, lse_ref, kbuf, vbuf, *, dilation, steps):
    blk = ATT_BLOCK
    m = ATT_SPAN // (blk * dilation)
    rows = m * blk
    n = pl.program_id(1)

    @pl.when(n == 0)
    def _():
        kbuf[0:blk, :] = jnp.zeros((blk, kbuf.shape[1]), kbuf.dtype)
        vbuf[0:blk, :] = jnp.zeros((blk, vbuf.shape[1]), vbuf.dtype)

    kbuf[blk:blk + rows, :] = k_ref[0]
    vbuf[blk:blk + rows, :] = v_ref[0]

    qi = lax.broadcasted_iota(jnp.int32, (blk, 2 * blk), 0)
    kj = lax.broadcasted_iota(jnp.int32, (blk, 2 * blk), 1)
    off = qi + blk - kj
    band = (off >= 0) & (off <= steps)
    first_lo = jnp.where(n > 0, 0, blk)
    band_first = band & (kj >= first_lo)
    lane = lax.broadcasted_iota(jnp.int32, (blk, HEAD_DIM), 1)

    for i in range(m):
        mask = band_first if i == 0 else band
        for r in range(dilation):
            lse_tile = jnp.zeros((blk, HEAD_DIM), F32)
            for h in range(HEADS_PER_GROUP):
                c0 = r * GROUP_W + h * HEAD_DIM
                qt = q_ref[0, i * blk:(i + 1) * blk, c0:c0 + HEAD_DIM]
                kt = kbuf[i * blk:(i + 2) * blk, c0:c0 + HEAD_DIM]
                vt = vbuf[i * blk:(i + 2) * blk, c0:c0 + HEAD_DIM]
                s = lax.dot_general(qt, kt, (((1,), (1,)), ((), ())), preferred_element_type=F32)
                s = jnp.where(mask, s, NEG_BIG)
                mx = jnp.max(s, axis=-1, keepdims=True)
                p = jnp.exp(s - mx)
                den = jnp.sum(p, axis=-1, keepdims=True)
                o = jnp.dot(p.astype(BF16), vt, preferred_element_type=F32) / den
                o_ref[0, i * blk:(i + 1) * blk, c0:c0 + HEAD_DIM] = o.astype(BF16)
                lse_tile = jnp.where(lane == h, mx + jnp.log(den), lse_tile)
            lse_ref[0, i * blk:(i + 1) * blk, r * HEAD_DIM:(r + 1) * HEAD_DIM] = lse_tile

    kbuf[0:blk, :] = kbuf[rows:rows + blk, :]
    vbuf[0:blk, :] = vbuf[rows:rows + blk, :]


def _dil_att(q, k, v, *, window, dilation):
    d = dilation
    B = q.shape[0]
    S = q.shape[1] * d
    rows = ATT_SPAN // d
    wide = d * GROUP_W
    grid = (B, S // ATT_SPAN)
    spec = pl.BlockSpec((1, rows, wide), lambda b, n: (b, n, 0))
    lse_spec = pl.BlockSpec((1, rows, d * HEAD_DIM), lambda b, n: (b, n, 0))
    return pl.pallas_call(
        functools.partial(_dil_att_kernel, dilation=d, steps=window // d),
        grid=grid,
        in_specs=[spec, spec, spec],
        out_specs=[spec, lse_spec],
        out_shape=[jax.ShapeDtypeStruct((B, S // d, wide), BF16),
                   jax.ShapeDtypeStruct((B, S // d, d * HEAD_DIM), F32)],
        scratch_shapes=[pltpu.VMEM((rows + ATT_BLOCK, wide), BF16),
                        pltpu.VMEM((rows + ATT_BLOCK, wide), BF16)],
        compiler_params=pltpu.CompilerParams(
            dimension_semantics=("arbitrary", "arbitrary"),
            vmem_limit_bytes=V7X_VMEM_LIMIT),
        name=f"dil_att_d{d}",
    )(q, k, v)


def _att_out_kernel(o0_ref, o1_ref, o2_ref, l0_ref, l1_ref, l2_ref, z_ref, h_ref,
                    wout_ref, lng_ref, lnb_ref, out_ref, o_scr, l_scr, *, tile):
    for g, ((_, d), o_ref, l_ref) in enumerate(zip(DIL_GROUPS, (o0_ref, o1_ref, o2_ref),
                                                   (l0_ref, l1_ref, l2_ref))):
        for r in range(d):
            rows = pl.ds(r, tile // d, stride=d) if d > 1 else slice(None)
            for h in range(HEADS_PER_GROUP):
                c0 = r * GROUP_W + h * HEAD_DIM
                o_scr[g * HEADS_PER_GROUP + h, rows, :] = o_ref[0, :, c0:c0 + HEAD_DIM].astype(F32)
            l_scr[g, rows, :] = l_ref[0, :, r * HEAD_DIM:(r + 1) * HEAD_DIM]
    lses = [l_scr[g] for g in range(N_GROUPS)]
    mx = jnp.maximum(jnp.maximum(lses[0], lses[1]), lses[2])
    es = [jnp.exp(l - mx) for l in lses]
    inv = 1.0 / (es[0] + es[1] + es[2])
    wts = [e * inv for e in es]
    merged = []
    for h in range(HEADS_PER_GROUP):
        sl = slice(h * HEAD_DIM, (h + 1) * HEAD_DIM)
        acc = None
        for g in range(N_GROUPS):
            t = wts[g][:, h:h + 1] * o_scr[g * HEADS_PER_GROUP + h]
            acc = t if acc is None else acc + t
        zh = z_ref[0, :, sl].astype(F32)
        merged.append((acc * _silu(zh)).astype(BF16))
    y = jnp.dot(jnp.concatenate(merged, axis=1), wout_ref[...], preferred_element_type=F32)
    out_ref[0] = _layer_norm(ALPHA * h_ref[0] + y, lng_ref[...], lnb_ref[...])


def _att_out(os_, lses, z, h, w_out, ln_g, ln_b, *, tile):
    B, S, D = h.shape
    grid = (B, S // tile)
    tok = lambda w: pl.BlockSpec((1, tile, w), lambda b, s: (b, s, 0))
    dil = lambda d, w: pl.BlockSpec((1, tile // d, d * w), lambda b, s: (b, s, 0))
    ds_ = [d for _, d in DIL_GROUPS]
    return pl.pallas_call(
        functools.partial(_att_out_kernel, tile=tile),
        grid=grid,
        in_specs=[dil(d, GROUP_W) for d in ds_] + [dil(d, HEAD_DIM) for d in ds_] + [
                  tok(GROUP_W), tok(D),
                  _resident(w_out.shape), _resident(ln_g.shape), _resident(ln_b.shape)],
        out_specs=tok(D),
        out_shape=jax.ShapeDtypeStruct((B, S, D), F32),
        scratch_shapes=[pltpu.VMEM((N_GROUPS * HEADS_PER_GROUP, tile, HEAD_DIM), F32),
                        pltpu.VMEM((N_GROUPS, tile, HEAD_DIM), F32)],
        compiler_params=pltpu.CompilerParams(
            dimension_semantics=("arbitrary", "arbitrary"),
            vmem_limit_bytes=V7X_VMEM_LIMIT),
        name="att_out",
    )(*os_, *lses, z, h, w_out, ln_g, ln_b)


def _rope_tables(seq_len):
    inv = 1.0 / (ROPE_THETA ** (jnp.arange(0, HEAD_DIM, 2, dtype=F32) / HEAD_DIM))
    ang = jnp.arange(seq_len, dtype=F32)[:, None] * inv[None, :]
    ang = jnp.concatenate([ang, ang], -1)
    sign = jnp.where(jnp.arange(HEAD_DIM) < HEAD_DIM // 2, -1.0, 1.0).astype(F32)
    return jnp.cos(ang), jnp.sin(ang) * sign[None, :]


def _pad_lanes(v, width=128):
    return jnp.zeros((1, width), F32).at[0, :v.shape[0]].set(v.astype(F32))


def kernel(x, ln_g, ln_b, gdn_w_in, gdn_conv_w, gdn_a_log, gdn_dt_bias, gdn_norm_g, gdn_w_out,
           kv_w, att_w_in, att_w_out):
    B, S, D = x.shape
    w_in = gdn_w_in[0]
    w_main = w_in[:, :GDN_CONV_W + GDN_V_W].astype(BF16)
    n_main = GDN_CONV_W + GDN_V_W
    w_ba = jnp.zeros((D, 256), F32)
    w_ba = w_ba.at[:, 0:GDN_HEADS].set(w_in[:, n_main:n_main + GDN_HEADS])
    w_ba = w_ba.at[:, 128:128 + GDN_HEADS].set(w_in[:, n_main + GDN_HEADS:]).astype(BF16)
    q, k, v, z, scal = _gdn_in(x, w_main, w_ba, gdn_conv_w[0], _pad_lanes(gdn_a_log[0]),
                               _pad_lanes(gdn_dt_bias[0]), tile=256)
    h1 = _gdn_core(q, k, v, z, scal, x, gdn_w_out[0].astype(BF16), gdn_norm_g[0][None, :],
                   ln_g[0][None, :], ln_b[0][None, :], tile=256)
    w_att = jnp.concatenate([att_w_in[0][:, :ATT_Q_W], kv_w, att_w_in[0][:, ATT_Q_W:]], axis=1).astype(BF16)
    cos, sin_s = _rope_tables(S)
    outs = _att_in(h1, w_att, cos, sin_s, tile=256)
    qs, ks, vs, za = outs[0:3], outs[3:6], outs[6:9], outs[9]
    os_, lses = [], []
    for g, (window, dilation) in enumerate(DIL_GROUPS):
        o, lse = _dil_att(qs[g], ks[g], vs[g], window=window, dilation=dilation)
        os_.append(o)
        lses.append(lse)
    return _att_out(os_, lses, za, h1, att_w_out[0].astype(BF16), ln_g[1][None, :], ln_b[1][None, :],
                    tile=512)
```

```python
import functools
import math

import jax
import jax.numpy as jnp
from jax import lax
from jax.experimental import pallas as pl
from jax.experimental.pallas import tpu as pltpu

F32 = jnp.float32
BF16 = jnp.bfloat16

D_MODEL = 1024
DEPTH = 2
ALPHA = (2.0 * DEPTH) ** 0.25
LN_EPS = 1e-5
RMS_EPS = 1e-6

GDN_HEADS = 8
GDN_DK = 128
GDN_DV = 256
GDN_QK_W = GDN_HEADS * GDN_DK
GDN_V_W = GDN_HEADS * GDN_DV
GDN_CONV_W = 2 * GDN_QK_W + GDN_V_W
CONV_K = 4
CHUNK = 64
CONV_GROUP = 1024

DIL_GROUPS = ((128, 1), (512, 4), (2048, 16))
N_GROUPS = 3
HEADS_PER_GROUP = 4
HEAD_DIM = 128
GROUP_W = HEADS_PER_GROUP * HEAD_DIM
ATT_Q_W = N_GROUPS * GROUP_W
ATT_BLOCK = 128
ATT_SPAN = 2048
ROPE_THETA = 10000.0

SC_GC, SC_BETA, SC_EGC, SC_EGR, SC_BEGE, SC_EGL = 0, 8, 16, 24, 32, 40

V7X_VMEM_LIMIT = 56 * 1024 * 1024
NEG_BIG = -1e30


def _resident(shape):
    nd = len(shape)
    return pl.BlockSpec(shape, lambda *_: (0,) * nd, pipeline_mode=pl.Buffered(1))


def _layer_norm(r, g, b):
    mu = jnp.mean(r, axis=-1, keepdims=True)
    c = r - mu
    var = jnp.mean(c * c, axis=-1, keepdims=True)
    return c * lax.rsqrt(var + LN_EPS) * g + b


def _silu(x):
    h = 0.5 * x
    return h + h * jnp.tanh(h)


def _split3(x):
    hi = x.astype(BF16)
    r1 = x - hi.astype(F32)
    mid = r1.astype(BF16)
    lo = (r1 - mid.astype(F32)).astype(BF16)
    return hi, mid, lo


def _gdn_in_kernel(x_ref, w_ref, wba_ref, cw_ref, alog_ref, dtb_ref,
                   q_ref, k_ref, v_ref, scal_ref,
                   pc_ref, tail_ref, *, tile):
    xb = x_ref[0].astype(BF16)

    @pl.when(pl.program_id(1) == 0)
    def _():
        tail_ref[...] = jnp.zeros_like(tail_ref)

    hdr = 8

    def conv_slab(g):
        c0 = g * CONV_GROUP
        cols = slice(c0, c0 + CONV_GROUP)
        p = jnp.dot(xb, w_ref[:, cols], preferred_element_type=F32)
        pc_ref[0:hdr, :] = tail_ref[:, cols]
        pc_ref[hdr:hdr + tile, :] = p
        tail_ref[:, cols] = p[tile - hdr:, :]
        y = cw_ref[CONV_K - 1:CONV_K, cols] * p
        for j in range(CONV_K - 1):
            off = hdr - (CONV_K - 1) + j
            y = y + cw_ref[j:j + 1, cols] * pc_ref[off:off + tile, :]
        return _silu(y)

    def l2n(y, scale):
        outs = []
        for h in range(GDN_HEADS):
            yh = y[:, h * GDN_DK:(h + 1) * GDN_DK]
            ss = jnp.sum(yh * yh, axis=-1, keepdims=True)
            outs.append(yh * (lax.rsqrt(ss + RMS_EPS) * scale))
        return outs

    for h, qh in enumerate(l2n(conv_slab(0), GDN_DK ** -0.5)):
        q_ref[0, :, h * GDN_DK:(h + 1) * GDN_DK] = qh.astype(BF16)
    for h, kh in enumerate(l2n(conv_slab(1), 1.0)):
        k_ref[0, :, h * GDN_DK:(h + 1) * GDN_DK] = kh.astype(BF16)
    for g in range(2, GDN_CONV_W // CONV_GROUP):
        c0 = (g - 2) * CONV_GROUP
        v_ref[0, :, c0:c0 + CONV_GROUP] = conv_slab(g).astype(BF16)

    ba = jnp.dot(xb, wba_ref[...], preferred_element_type=F32)
    beta = jax.nn.sigmoid(ba[:, 0:128])
    a_in = ba[:, 128:256] + dtb_ref[...]
    softplus = jnp.maximum(a_in, 0.0) + jnp.log1p(jnp.exp(-jnp.abs(a_in)))
    g = -jnp.exp(alog_ref[...]) * softplus

    row = lax.broadcasted_iota(jnp.int32, (tile, tile), 0)
    col = lax.broadcasted_iota(jnp.int32, (tile, tile), 1)
    shift = CHUNK.bit_length() - 1
    same = jnp.right_shift(row, shift) == jnp.right_shift(col, shift)
    l_all = jnp.where(same, 1.0, 0.0)
    l_cum = jnp.where(col <= row, l_all, 0.0).astype(BF16)
    l_all = l_all.astype(BF16)
    lmat = jnp.concatenate([l_cum, l_all], axis=0)
    acc = jnp.zeros((2 * tile, 128), F32)
    for piece in _split3(g):
        acc = acc + jnp.dot(lmat, piece, preferred_element_type=F32)
    gc = acc[:tile]
    gl = acc[tile:]
    egc = jnp.exp(gc)
    egr = jnp.exp(gl - gc)
    egl = jnp.exp(gl)
    bege = beta * egc

    lane = lax.broadcasted_iota(jnp.int32, (tile, 128), 1)
    packed = jnp.zeros((tile, 128), F32)
    for slot, val in ((SC_GC, gc), (SC_BETA, beta), (SC_EGC, egc), (SC_EGR, egr),
                      (SC_BEGE, bege), (SC_EGL, egl)):
        shifted = val if slot == 0 else pltpu.roll(val, slot, axis=1)
        packed = jnp.where((lane >= slot) & (lane < slot + GDN_HEADS), shifted, packed)
    scal_ref[0] = packed


def _gdn_in(x, w_main, w_ba, conv_w, alog, dtb, *, tile):
    B, S, D = x.shape
    grid = (B, S // tile)
    tok = lambda w: pl.BlockSpec((1, tile, w), lambda b, s: (b, s, 0))
    return pl.pallas_call(
        functools.partial(_gdn_in_kernel, tile=tile),
        grid=grid,
        in_specs=[tok(D), _resident(w_main.shape), _resident(w_ba.shape),
                  _resident(conv_w.shape), _resident(alog.shape), _resident(dtb.shape)],
        out_specs=[tok(GDN_QK_W), tok(GDN_QK_W), tok(GDN_V_W), tok(128)],
        out_shape=[jax.ShapeDtypeStruct((B, S, GDN_QK_W), BF16),
                   jax.ShapeDtypeStruct((B, S, GDN_QK_W), BF16),
                   jax.ShapeDtypeStruct((B, S, GDN_V_W), BF16),
                   jax.ShapeDtypeStruct((B, S, 128), F32)],
        scratch_shapes=[pltpu.VMEM((tile + 8, CONV_GROUP), F32),
                        pltpu.VMEM((8, GDN_CONV_W), F32)],
        compiler_params=pltpu.CompilerParams(
            dimension_semantics=("arbitrary", "arbitrary"),
            vmem_limit_bytes=V7X_VMEM_LIMIT),
        name="gdn_in",
    )(x, w_main, w_ba, conv_w, alog, dtb)


def _gdn_core_kernel(q_ref, k_ref, v_ref, scal_ref, x_ref, wz_ref, wout_ref,
                     ng_ref, lng_ref, lnb_ref, out_ref, state_ref, o_ref, a_scr, z_scr, *, tile):
    C = CHUNK
    z_slab = 512

    def gate_proj():
        xb = x_ref[0].astype(BF16)
        for c0 in range(0, GDN_V_W, z_slab):
            z_scr[:, c0:c0 + z_slab] = jnp.dot(xb, wz_ref[:, c0:c0 + z_slab],
                                               preferred_element_type=F32).astype(BF16)
            yield

    @pl.when(pl.program_id(1) == 0)
    def _():
        state_ref[...] = jnp.zeros_like(state_ref)

    ri = lax.broadcasted_iota(jnp.int32, (C, 2 * C), 0)
    li = lax.broadcasted_iota(jnp.int32, (C, 2 * C), 1)
    hi_half = li >= C
    incl = hi_half & (ri >= li - C)
    strict = hi_half & (ri > li - C)
    eye_lo = jnp.where(li == ri, 1.0, 0.0).astype(F32)
    zeros_k = jnp.zeros((C, GDN_DK), BF16)
    zeros_w = jnp.zeros((C, 2 * C), BF16)
    zeros_v = jnp.zeros((C, GDN_DV), BF16)

    heads = range(GDN_HEADS)

    A_TB, A_WK, A_QK, A_KG = range(4)

    def intra(c):
        r0, slot = c * C, c % 2
        sc = scal_ref[0, r0:r0 + C, :]
        sct = jnp.concatenate([sc, sc], axis=0).T
        col = lambda s, h: sc[:, s + h:s + h + 1]
        row = lambda s, h: sct[s + h:s + h + 1, :]
        qs = [q_ref[0, r0:r0 + C, h * GDN_DK:(h + 1) * GDN_DK] for h in heads]
        ks = [k_ref[0, r0:r0 + C, h * GDN_DK:(h + 1) * GDN_DK] for h in heads]
        qk2 = [lax.dot_general(jnp.concatenate([qs[h], ks[h]], axis=0),
                               jnp.concatenate([zeros_k, ks[h]], axis=0),
                               (((1,), (1,)), ((), ())), preferred_element_type=F32) for h in heads]
        yield
        decay = [jnp.exp(jnp.where(incl, col(SC_GC, h) - row(SC_GC, h), 0.0)) for h in heads]
        for h in heads:
            a_scr[slot, A_QK, h] = jnp.where(incl, qk2[h][:C] * decay[h], 0.0).astype(BF16)
            a_scr[slot, A_KG, h] = (ks[h].astype(F32) * col(SC_EGR, h)).astype(BF16)
        w = [jnp.where(strict, -(qk2[h][C:] * decay[h] * col(SC_BETA, h)), 0.0) + eye_lo for h in heads]
        for _ in range(6):
            wb = [w[h].astype(BF16) for h in heads]
            y = [jnp.dot(wb[h], jnp.concatenate([zeros_w, wb[h]], axis=0), preferred_element_type=F32)
                 for h in heads]
            w = [y[h] + jnp.where(hi_half, 0.0, w[h]) for h in heads]
            yield
        tbg = [(w[h] * row(SC_BEGE, h)).astype(BF16)[:, :C] for h in heads]
        wk = [jnp.dot(tbg[h], ks[h], preferred_element_type=F32) for h in heads]
        for h in heads:
            a_scr[slot, A_TB, h] = (w[h] * row(SC_BETA, h)).astype(BF16)
            a_scr[slot, A_WK, h] = wk[h].astype(BF16)
        yield

    def inter(c, state):
        r0, slot = c * C, c % 2
        sc = scal_ref[0, r0:r0 + C, :]
        col = lambda s, h: sc[:, s + h:s + h + 1]
        qs = [q_ref[0, r0:r0 + C, h * GDN_DK:(h + 1) * GDN_DK] for h in heads]
        vs = [v_ref[0, r0:r0 + C, h * GDN_DV:(h + 1) * GDN_DV] for h in heads]
        r = [jnp.dot(jnp.concatenate([a_scr[slot, A_WK, h], qs[h]], axis=0), state[h].astype(BF16),
                     preferred_element_type=F32) for h in heads]
        u = [jnp.dot(a_scr[slot, A_TB, h][:, :C], vs[h], preferred_element_type=F32) for h in heads]
        yield
        vnb = [(u[h] - r[h][:C]).astype(BF16) for h in heads]
        o2 = [jnp.dot(a_scr[slot, A_QK, h], jnp.concatenate([zeros_v, vnb[h]], axis=0),
                      preferred_element_type=F32) for h in heads]
        ds = [lax.dot_general(a_scr[slot, A_KG, h], vnb[h], (((0,), (0,)), ((), ())),
                              preferred_element_type=F32) for h in heads]
        yield
        for h in heads:
            egl_c = col(SC_EGL, h)
            state[h] = state[h] * jnp.concatenate([egl_c, egl_c], axis=0) + ds[h]
            o_ref[r0:r0 + C, h * GDN_DV:(h + 1) * GDN_DV] = r[h][C:] * col(SC_EGC, h) + o2[h]
        yield

    n_chunks = tile // C
    state = [state_ref[h] for h in heads]
    gate = gate_proj()
    for i, _ in enumerate(intra(0)):
        if i in (2, 5):
            next(gate, None)
    for c in range(n_chunks):
        a = intra(c + 1) if c + 1 < n_chunks else gate
        b = inter(c, state)
        for who in "abaaabaaaba":
            next(a if who == "a" else b, None)
    for _ in gate:
        pass
    for h in heads:
        state_ref[h] = state[h]

    gated = []
    for h in range(GDN_HEADS):
        oh = o_ref[:, h * GDN_DV:(h + 1) * GDN_DV]
        ms = jnp.mean(oh * oh, axis=-1, keepdims=True)
        on = oh * lax.rsqrt(ms + RMS_EPS) * ng_ref[...]
        zh = z_scr[:, h * GDN_DV:(h + 1) * GDN_DV].astype(F32)
        gated.append((on * _silu(zh)).astype(BF16))
    y = jnp.dot(jnp.concatenate(gated, axis=1), wout_ref[...], preferred_element_type=F32)
    out_ref[0] = _layer_norm(ALPHA * x_ref[0] + y, lng_ref[...], lnb_ref[...])


def _gdn_core(q, k, v, scal, x, w_z, w_out, norm_g, ln_g, ln_b, *, tile):
    B, S, D = x.shape
    grid = (B, S // tile)
    tok = lambda w: pl.BlockSpec((1, tile, w), lambda b, s: (b, s, 0))
    return pl.pallas_call(
        functools.partial(_gdn_core_kernel, tile=tile),
        grid=grid,
        in_specs=[tok(GDN_QK_W), tok(GDN_QK_W), tok(GDN_V_W), tok(128), tok(D),
                  _resident(w_z.shape), _resident(w_out.shape), _resident(norm_g.shape),
                  _resident(ln_g.shape), _resident(ln_b.shape)],
        out_specs=tok(D),
        out_shape=jax.ShapeDtypeStruct((B, S, D), F32),
        scratch_shapes=[pltpu.VMEM((GDN_HEADS, GDN_DK, GDN_DV), F32),
                        pltpu.VMEM((tile, GDN_V_W), F32),
                        pltpu.VMEM((2, 4, GDN_HEADS, CHUNK, 2 * CHUNK), BF16),
                        pltpu.VMEM((tile, GDN_V_W), BF16)],
        compiler_params=pltpu.CompilerParams(
            dimension_semantics=("arbitrary", "arbitrary"),
            vmem_limit_bytes=V7X_VMEM_LIMIT),
        name="gdn_core",
    )(q, k, v, scal, x, w_z, w_out, norm_g, ln_g, ln_b)


def _att_in_kernel(h_ref, w_ref, cos_ref, sin_ref, *refs, tile):
    out_refs, stage_ref = refs[:-1], refs[-1]
    q_refs, k_refs, v_refs, z_ref = out_refs[0:3], out_refs[3:6], out_refs[6:9], out_refs[9]
    hb = h_ref[0].astype(BF16)
    cos = cos_ref[...]
    sin_s = sin_ref[...]

    def rope(t, scale):
        outs = []
        for h in range(HEADS_PER_GROUP):
            th = t[:, h * HEAD_DIM:(h + 1) * HEAD_DIM]
            r = th * cos + pltpu.roll(th, HEAD_DIM // 2, axis=1) * sin_s
            outs.append(r * scale if scale != 1.0 else r)
        return jnp.concatenate(outs, axis=1)

    def emit(out_ref, val, d):
        if d == 1:
            out_ref[0] = val.astype(BF16)
            return
        for h in range(HEADS_PER_GROUP):
            stage_ref[h] = val[:, h * HEAD_DIM:(h + 1) * HEAD_DIM]
        for r in range(d):
            for h in range(HEADS_PER_GROUP):
                c0 = r * GROUP_W + h * HEAD_DIM
                out_ref[0, :, c0:c0 + HEAD_DIM] = (
                    stage_ref[h, pl.ds(r, tile // d, stride=d), :].astype(BF16))

    for g, (_, d) in enumerate(DIL_GROUPS):
        c0 = g * GROUP_W
        qg = jnp.dot(hb, w_ref[:, c0:c0 + GROUP_W], preferred_element_type=F32)
        emit(q_refs[g], rope(qg, HEAD_DIM ** -0.5), d)
        c0 = ATT_Q_W + g * GROUP_W
        kg = jnp.dot(hb, w_ref[:, c0:c0 + GROUP_W], preferred_element_type=F32)
        emit(k_refs[g], rope(kg, 1.0), d)
        c0 = 2 * ATT_Q_W + g * GROUP_W
        emit(v_refs[g], jnp.dot(hb, w_ref[:, c0:c0 + GROUP_W], preferred_element_type=F32), d)
    c0 = 3 * ATT_Q_W
    z_ref[0] = jnp.dot(hb, w_ref[:, c0:c0 + GROUP_W], preferred_element_type=F32).astype(BF16)


def _att_in(h, w_att, cos, sin_s, *, tile):
    B, S, D = h.shape
    grid = (B, S // tile)
    tok = lambda w: pl.BlockSpec((1, tile, w), lambda b, s: (b, s, 0))
    tab = pl.BlockSpec((tile, HEAD_DIM), lambda b, s: (s, 0))
    dil = lambda d: pl.BlockSpec((1, tile // d, d * GROUP_W), lambda b, s: (b, s, 0))
    dil_shape = lambda d: jax.ShapeDtypeStruct((B, S // d, d * GROUP_W), BF16)
    ds_ = [d for _, d in DIL_GROUPS]
    return pl.pallas_call(
        functools.partial(_att_in_kernel, tile=tile),
        grid=grid,
        in_specs=[tok(D), _resident(w_att.shape), tab, tab],
        out_specs=[dil(d) for d in ds_] * 3 + [tok(GROUP_W)],
        out_shape=[dil_shape(d) for d in ds_] * 3 + [jax.ShapeDtypeStruct((B, S, GROUP_W), BF16)],
        scratch_shapes=[pltpu.VMEM((HEADS_PER_GROUP, tile, HEAD_DIM), F32)],
        compiler_params=pltpu.CompilerParams(
            dimension_semantics=("arbitrary", "arbitrary"),
            vmem_limit_bytes=V7X_VMEM_LIMIT),
        name="att_in",
    )(h, w_att, cos, sin_s)


def _dil_att_kernel(q_ref, k_ref, v_ref, o_ref, lse_ref, kbuf, vbuf, *, dilation, steps):
    blk = ATT_BLOCK
    m = ATT_SPAN // (blk * dilation)
    rows = m * blk
    n = pl.program_id(1)

    @pl.when(n == 0)
    def _():
        kbuf[0:blk, :] = jnp.zeros((blk, kbuf.shape[1]), kbuf.dtype)
        vbuf[0:blk, :] = jnp.zeros((blk, vbuf.shape[1]), vbuf.dtype)

    kbuf[blk:blk + rows, :] = k_ref[0]
    vbuf[blk:blk + rows, :] = v_ref[0]

    qi = lax.broadcasted_iota(jnp.int32, (blk, 2 * blk), 0)
    kj = lax.broadcasted_iota(jnp.int32, (blk, 2 * blk), 1)
    off = qi + blk - kj
    band = (off >= 0) & (off <= steps)
    first_lo = jnp.where(n > 0, 0, blk)
    band_first = band & (kj >= first_lo)
    lane = lax.broadcasted_iota(jnp.int32, (blk, HEAD_DIM), 1)

    for i in range(m):
        mask = band_first if i == 0 else band
        for r in range(dilation):
            lse_tile = jnp.zeros((blk, HEAD_DIM), F32)
            for h in range(HEADS_PER_GROUP):
                c0 = r * GROUP_W + h * HEAD_DIM
                qt = q_ref[0, i * blk:(i + 1) * blk, c0:c0 + HEAD_DIM]
                kt = kbuf[i * blk:(i + 2) * blk, c0:c0 + HEAD_DIM]
                vt = vbuf[i * blk:(i + 2) * blk, c0:c0 + HEAD_DIM]
                s = lax.dot_general(qt, kt, (((1,), (1,)), ((), ())), preferred_element_type=F32)
                s = jnp.where(mask, s, NEG_BIG)
                mx = jnp.max(s, axis=-1, keepdims=True)
                p = jnp.exp(s - mx)
                den = jnp.sum(p, axis=-1, keepdims=True)
                o = jnp.dot(p.astype(BF16), vt, preferred_element_type=F32) / den
                o_ref[0, i * blk:(i + 1) * blk, c0:c0 + HEAD_DIM] = o.astype(BF16)
                lse_tile = jnp.where(lane == h, mx + jnp.log(den), lse_tile)
            lse_ref[0, i * blk:(i + 1) * blk, r * HEAD_DIM:(r + 1) * HEAD_DIM] = lse_tile

    kbuf[0:blk, :] = kbuf[rows:rows + blk, :]
    vbuf[0:blk, :] = vbuf[rows:rows + blk, :]


def _dil_att(q, k, v, *, window, dilation):
    d = dilation
    B = q.shape[0]
    S = q.shape[1] * d
    rows = ATT_SPAN // d
    wide = d * GROUP_W
    grid = (B, S // ATT_SPAN)
    spec = pl.BlockSpec((1, rows, wide), lambda b, n: (b, n, 0))
    lse_spec = pl.BlockSpec((1, rows, d * HEAD_DIM), lambda b, n: (b, n, 0))
    return pl.pallas_call(
        functools.partial(_dil_att_kernel, dilation=d, steps=window // d),
        grid=grid,
        in_specs=[spec, spec, spec],
        out_specs=[spec, lse_spec],
        out_shape=[jax.ShapeDtypeStruct((B, S // d, wide), BF16),
                   jax.ShapeDtypeStruct((B, S // d, d * HEAD_DIM), F32)],
        scratch_shapes=[pltpu.VMEM((rows + ATT_BLOCK, wide), BF16),
                        pltpu.VMEM((rows + ATT_BLOCK, wide), BF16)],
        compiler_params=pltpu.CompilerParams(
            dimension_semantics=("arbitrary", "arbitrary"),
            vmem_limit_bytes=V7X_VMEM_LIMIT),
        name=f"dil_att_d{d}",
    )(q, k, v)


def _att_out_kernel(o0_ref, o1_ref, o2_ref, l0_ref, l1_ref, l2_ref, z_ref, h_ref,
                    wout_ref, lng_ref, lnb_ref, out_ref, o_scr, l_scr, *, tile):
    for g, ((_, d), o_ref, l_ref) in enumerate(zip(DIL_GROUPS, (o0_ref, o1_ref, o2_ref),
                                                   (l0_ref, l1_ref, l2_ref))):
        for r in range(d):
            rows = pl.ds(r, tile // d, stride=d) if d > 1 else slice(None)
            for h in range(HEADS_PER_GROUP):
                c0 = r * GROUP_W + h * HEAD_DIM
                o_scr[g * HEADS_PER_GROUP + h, rows, :] = o_ref[0, :, c0:c0 + HEAD_DIM].astype(F32)
            l_scr[g, rows, :] = l_ref[0, :, r * HEAD_DIM:(r + 1) * HEAD_DIM]
    lses = [l_scr[g] for g in range(N_GROUPS)]
    mx = jnp.maximum(jnp.maximum(lses[0], lses[1]), lses[2])
    es = [jnp.exp(l - mx) for l in lses]
    inv = 1.0 / (es[0] + es[1] + es[2])
    wts = [e * inv for e in es]
    merged = []
    for h in range(HEADS_PER_GROUP):
        sl = slice(h * HEAD_DIM, (h + 1) * HEAD_DIM)
        acc = None
        for g in range(N_GROUPS):
            t = wts[g][:, h:h + 1] * o_scr[g * HEADS_PER_GROUP + h]
            acc = t if acc is None else acc + t
        zh = z_ref[0, :, sl].astype(F32)
        merged.append((acc * _silu(zh)).astype(BF16))
    y = jnp.dot(jnp.concatenate(merged, axis=1), wout_ref[...], preferred_element_type=F32)
    out_ref[0] = _layer_norm(ALPHA * h_ref[0] + y, lng_ref[...], lnb_ref[...])


def _att_out(os_, lses, z, h, w_out, ln_g, ln_b, *, tile):
    B, S, D = h.shape
    grid = (B, S // tile)
    tok = lambda w: pl.BlockSpec((1, tile, w), lambda b, s: (b, s, 0))
    dil = lambda d, w: pl.BlockSpec((1, tile // d, d * w), lambda b, s: (b, s, 0))
    ds_ = [d for _, d in DIL_GROUPS]
    return pl.pallas_call(
        functools.partial(_att_out_kernel, tile=tile),
        grid=grid,
        in_specs=[dil(d, GROUP_W) for d in ds_] + [dil(d, HEAD_DIM) for d in ds_] + [
                  tok(GROUP_W), tok(D),
                  _resident(w_out.shape), _resident(ln_g.shape), _resident(ln_b.shape)],
        out_specs=tok(D),
        out_shape=jax.ShapeDtypeStruct((B, S, D), F32),
        scratch_shapes=[pltpu.VMEM((N_GROUPS * HEADS_PER_GROUP, tile, HEAD_DIM), F32),
                        pltpu.VMEM((N_GROUPS, tile, HEAD_DIM), F32)],
        compiler_params=pltpu.CompilerParams(
            dimension_semantics=("arbitrary", "arbitrary"),
            vmem_limit_bytes=V7X_VMEM_LIMIT),
        name="att_out",
    )(*os_, *lses, z, h, w_out, ln_g, ln_b)


def _rope_tables(seq_len):
    inv = 1.0 / (ROPE_THETA ** (jnp.arange(0, HEAD_DIM, 2, dtype=F32) / HEAD_DIM))
    ang = jnp.arange(seq_len, dtype=F32)[:, None] * inv[None, :]
    ang = jnp.concatenate([ang, ang], -1)
    sign = jnp.where(jnp.arange(HEAD_DIM) < HEAD_DIM // 2, -1.0, 1.0).astype(F32)
    return jnp.cos(ang), jnp.sin(ang) * sign[None, :]


def _pad_lanes(v, width=128):
    return jnp.zeros((1, width), F32).at[0, :v.shape[0]].set(v.astype(F32))


def kernel(x, ln_g, ln_b, gdn_w_in, gdn_conv_w, gdn_a_log, gdn_dt_bias, gdn_norm_g, gdn_w_out,
           kv_w, att_w_in, att_w_out):
    B, S, D = x.shape
    w_in = gdn_w_in[0]
    w_main = w_in[:, :GDN_CONV_W].astype(BF16)
    n_main = GDN_CONV_W + GDN_V_W
    w_z = w_in[:, GDN_CONV_W:n_main].astype(BF16)
    w_ba = jnp.zeros((D, 256), F32)
    w_ba = w_ba.at[:, 0:GDN_HEADS].set(w_in[:, n_main:n_main + GDN_HEADS])
    w_ba = w_ba.at[:, 128:128 + GDN_HEADS].set(w_in[:, n_main + GDN_HEADS:]).astype(BF16)
    q, k, v, scal = _gdn_in(x, w_main, w_ba, gdn_conv_w[0], _pad_lanes(gdn_a_log[0]),
                            _pad_lanes(gdn_dt_bias[0]), tile=256)
    h1 = _gdn_core(q, k, v, scal, x, w_z, gdn_w_out[0].astype(BF16), gdn_norm_g[0][None, :],
                   ln_g[0][None, :], ln_b[0][None, :], tile=256)
    w_att = jnp.concatenate([att_w_in[0][:, :ATT_Q_W], kv_w, att_w_in[0][:, ATT_Q_W:]], axis=1).astype(BF16)
    cos, sin_s = _rope_tables(S)
    outs = _att_in(h1, w_att, cos, sin_s, tile=256)
    qs, ks, vs, za = outs[0:3], outs[3:6], outs[6:9], outs[9]
    os_, lses = [], []
    for g, (window, dilation) in enumerate(DIL_GROUPS):
        o, lse = _dil_att(qs[g], ks[g], vs[g], window=window, dilation=dilation)
        os_.append(o)
        lses.append(lse)
    return _att_out(os_, lses, za, h1, att_w_out[0].astype(BF16), ln_g[1][None, :], ln_b[1][None, :],
                    tile=512)
```

```python
import functools
import math

import jax
import jax.numpy as jnp
from jax import lax
from jax.experimental import pallas as pl
from jax.experimental.pallas import tpu as pltpu

F32 = jnp.float32
BF16 = jnp.bfloat16

D_MODEL = 1024
DEPTH = 2
ALPHA = (2.0 * DEPTH) ** 0.25
LN_EPS = 1e-5
RMS_EPS = 1e-6

GDN_HEADS = 8
GDN_DK = 128
GDN_DV = 256
GDN_QK_W = GDN_HEADS * GDN_DK
GDN_V_W = GDN_HEADS * GDN_DV
GDN_CONV_W = 2 * GDN_QK_W + GDN_V_W
CONV_K = 4
CHUNK = 64
CONV_GROUP = 1024

DIL_GROUPS = ((128, 1), (512, 4), (2048, 16))
N_GROUPS = 3
HEADS_PER_GROUP = 4
HEAD_DIM = 128
GROUP_W = HEADS_PER_GROUP * HEAD_DIM
ATT_Q_W = N_GROUPS * GROUP_W
ATT_BLOCK = 128
ATT_SPAN = 2048
ROPE_THETA = 10000.0

SC_GC, SC_BETA, SC_EGC, SC_EGR, SC_BEGE, SC_EGL = 0, 8, 16, 24, 32, 40

V7X_VMEM_LIMIT = 56 * 1024 * 1024
NEG_BIG = -1e30


def _resident(shape):
    nd = len(shape)
    return pl.BlockSpec(shape, lambda *_: (0,) * nd, pipeline_mode=pl.Buffered(1))


def _layer_norm(r, g, b):
    mu = jnp.mean(r, axis=-1, keepdims=True)
    c = r - mu
    var = jnp.mean(c * c, axis=-1, keepdims=True)
    return c * lax.rsqrt(var + LN_EPS) * g + b


def _silu(x):
    h = 0.5 * x
    return h + h * jnp.tanh(h)


def _split3(x):
    hi = x.astype(BF16)
    r1 = x - hi.astype(F32)
    mid = r1.astype(BF16)
    lo = (r1 - mid.astype(F32)).astype(BF16)
    return hi, mid, lo


def _gdn_in_kernel(x_ref, w_ref, wba_ref, cw_ref, alog_ref, dtb_ref,
                   q_ref, k_ref, v_ref, scal_ref,
                   pc_ref, tail_ref, *, tile):
    xb = x_ref[0].astype(BF16)

    @pl.when(pl.program_id(1) == 0)
    def _():
        tail_ref[...] = jnp.zeros_like(tail_ref)

    hdr = 8

    def conv_slab(g):
        c0 = g * CONV_GROUP
        cols = slice(c0, c0 + CONV_GROUP)
        p = jnp.dot(xb, w_ref[:, cols], preferred_element_type=F32)
        pc_ref[0:hdr, :] = tail_ref[:, cols]
        pc_ref[hdr:hdr + tile, :] = p
        tail_ref[:, cols] = p[tile - hdr:, :]
        y = cw_ref[CONV_K - 1:CONV_K, cols] * p
        for j in range(CONV_K - 1):
            off = hdr - (CONV_K - 1) + j
            y = y + cw_ref[j:j + 1, cols] * pc_ref[off:off + tile, :]
        return _silu(y)

    def l2n(y, scale):
        outs = []
        for h in range(GDN_HEADS):
            yh = y[:, h * GDN_DK:(h + 1) * GDN_DK]
            ss = jnp.sum(yh * yh, axis=-1, keepdims=True)
            outs.append(yh * (lax.rsqrt(ss + RMS_EPS) * scale))
        return outs

    for h, qh in enumerate(l2n(conv_slab(0), GDN_DK ** -0.5)):
        q_ref[0, :, h * GDN_DK:(h + 1) * GDN_DK] = qh.astype(BF16)
    for h, kh in enumerate(l2n(conv_slab(1), 1.0)):
        k_ref[0, :, h * GDN_DK:(h + 1) * GDN_DK] = kh.astype(BF16)
    for g in range(2, GDN_CONV_W // CONV_GROUP):
        c0 = (g - 2) * CONV_GROUP
        v_ref[0, :, c0:c0 + CONV_GROUP] = conv_slab(g).astype(BF16)

    ba = jnp.dot(xb, wba_ref[...], preferred_element_type=F32)
    beta = jax.nn.sigmoid(ba[:, 0:128])
    a_in = ba[:, 128:256] + dtb_ref[...]
    softplus = jnp.maximum(a_in, 0.0) + jnp.log1p(jnp.exp(-jnp.abs(a_in)))
    g = -jnp.exp(alog_ref[...]) * softplus

    row = lax.broadcasted_iota(jnp.int32, (tile, tile), 0)
    col = lax.broadcasted_iota(jnp.int32, (tile, tile), 1)
    shift = CHUNK.bit_length() - 1
    same = jnp.right_shift(row, shift) == jnp.right_shift(col, shift)
    l_all = jnp.where(same, 1.0, 0.0)
    l_cum = jnp.where(col <= row, l_all, 0.0).astype(BF16)
    l_all = l_all.astype(BF16)
    lmat = jnp.concatenate([l_cum, l_all], axis=0)
    acc = jnp.zeros((2 * tile, 128), F32)
    for piece in _split3(g):
        acc = acc + jnp.dot(lmat, piece, preferred_element_type=F32)
    gc = acc[:tile]
    gl = acc[tile:]
    egc = jnp.exp(gc)
    egr = jnp.exp(gl - gc)
    egl = jnp.exp(gl)
    bege = beta * egc

    lane = lax.broadcasted_iota(jnp.int32, (tile, 128), 1)
    packed = jnp.zeros((tile, 128), F32)
    for slot, val in ((SC_GC, gc), (SC_BETA, beta), (SC_EGC, egc), (SC_EGR, egr),
                      (SC_BEGE, bege), (SC_EGL, egl)):
        shifted = val if slot == 0 else pltpu.roll(val, slot, axis=1)
        packed = jnp.where((lane >= slot) & (lane < slot + GDN_HEADS), shifted, packed)
    scal_ref[0] = packed


def _gdn_in(x, w_main, w_ba, conv_w, alog, dtb, *, tile):
    B, S, D = x.shape
    grid = (B, S // tile)
    tok = lambda w: pl.BlockSpec((1, tile, w), lambda b, s: (b, s, 0))
    return pl.pallas_call(
        functools.partial(_gdn_in_kernel, tile=tile),
        grid=grid,
        in_specs=[tok(D), _resident(w_main.shape), _resident(w_ba.shape),
                  _resident(conv_w.shape), _resident(alog.shape), _resident(dtb.shape)],
        out_specs=[tok(GDN_QK_W), tok(GDN_QK_W), tok(GDN_V_W), tok(128)],
        out_shape=[jax.ShapeDtypeStruct((B, S, GDN_QK_W), BF16),
                   jax.ShapeDtypeStruct((B, S, GDN_QK_W), BF16),
                   jax.ShapeDtypeStruct((B, S, GDN_V_W), BF16),
                   jax.ShapeDtypeStruct((B, S, 128), F32)],
        scratch_shapes=[pltpu.VMEM((tile + 8, CONV_GROUP), F32),
                        pltpu.VMEM((8, GDN_CONV_W), F32)],
        compiler_params=pltpu.CompilerParams(
            dimension_semantics=("arbitrary", "arbitrary"),
            vmem_limit_bytes=V7X_VMEM_LIMIT),
        name="gdn_in",
    )(x, w_main, w_ba, conv_w, alog, dtb)


def _gdn_core_kernel(q_ref, k_ref, v_ref, scal_ref, x_ref, wz_ref, wout_ref,
                     ng_ref, lng_ref, lnb_ref, out_ref, state_ref, o_ref, a_scr, z_scr, *, tile):
    C = CHUNK
    z_slab = 512

    def gate_proj():
        xb = x_ref[0].astype(BF16)
        for c0 in range(0, GDN_V_W, z_slab):
            z_scr[:, c0:c0 + z_slab] = jnp.dot(xb, wz_ref[:, c0:c0 + z_slab],
                                               preferred_element_type=F32).astype(BF16)
            yield

    @pl.when(pl.program_id(1) == 0)
    def _():
        state_ref[...] = jnp.zeros_like(state_ref)

    ri = lax.broadcasted_iota(jnp.int32, (C, 2 * C), 0)
    li = lax.broadcasted_iota(jnp.int32, (C, 2 * C), 1)
    hi_half = li >= C
    incl = hi_half & (ri >= li - C)
    strict = hi_half & (ri > li - C)
    eye_lo = jnp.where(li == ri, 1.0, 0.0).astype(F32)
    zeros_k = jnp.zeros((C, GDN_DK), BF16)
    zeros_w = jnp.zeros((C, 2 * C), BF16)
    zeros_v = jnp.zeros((C, GDN_DV), BF16)

    heads = range(GDN_HEADS)

    A_TB, A_WK, A_QK, A_KG = range(4)

    def intra(c):
        r0, slot = c * C, c % 2
        sc = scal_ref[0, r0:r0 + C, :]
        sct = jnp.concatenate([sc, sc], axis=0).T
        col = lambda s, h: sc[:, s + h:s + h + 1]
        row = lambda s, h: sct[s + h:s + h + 1, :]
        qs = [q_ref[0, r0:r0 + C, h * GDN_DK:(h + 1) * GDN_DK] for h in heads]
        ks = [k_ref[0, r0:r0 + C, h * GDN_DK:(h + 1) * GDN_DK] for h in heads]
        qk2 = [lax.dot_general(jnp.concatenate([qs[h], ks[h]], axis=0),
                               jnp.concatenate([zeros_k, ks[h]], axis=0),
                               (((1,), (1,)), ((), ())), preferred_element_type=F32) for h in heads]
        yield
        decay = [jnp.exp(jnp.where(incl, col(SC_GC, h) - row(SC_GC, h), 0.0)) for h in heads]
        for h in heads:
            a_scr[slot, A_QK, h] = jnp.where(incl, qk2[h][:C] * decay[h], 0.0).astype(BF16)
            a_scr[slot, A_KG, h] = (ks[h].astype(F32) * col(SC_EGR, h)).astype(BF16)
        w = [jnp.where(strict, -(qk2[h][C:] * decay[h] * col(SC_BETA, h)), 0.0) + eye_lo for h in heads]
        for _ in range(6):
            wb = [w[h].astype(BF16) for h in heads]
            y = [jnp.dot(wb[h], jnp.concatenate([zeros_w, wb[h]], axis=0), preferred_element_type=F32)
                 for h in heads]
            w = [y[h] + jnp.where(hi_half, 0.0, w[h]) for h in heads]
            yield
        tbg = [(w[h] * row(SC_BEGE, h)).astype(BF16)[:, :C] for h in heads]
        wk = [jnp.dot(tbg[h], ks[h], preferred_element_type=F32) for h in heads]
        for h in heads:
            a_scr[slot, A_TB, h] = (w[h] * row(SC_BETA, h)).astype(BF16)
            a_scr[slot, A_WK, h] = wk[h].astype(BF16)
        yield

    def inter(c, state):
        r0, slot = c * C, c % 2
        sc = scal_ref[0, r0:r0 + C, :]
        col = lambda s, h: sc[:, s + h:s + h + 1]
        qs = [q_ref[0, r0:r0 + C, h * GDN_DK:(h + 1) * GDN_DK] for h in heads]
        vs = [v_ref[0, r0:r0 + C, h * GDN_DV:(h + 1) * GDN_DV] for h in heads]
        r = [jnp.dot(jnp.concatenate([a_scr[slot, A_WK, h], qs[h]], axis=0), state[h].astype(BF16),
                     preferred_element_type=F32) for h in heads]
        u = [jnp.dot(a_scr[slot, A_TB, h][:, :C], vs[h], preferred_element_type=F32) for h in heads]
        yield
        vnb = [(u[h] - r[h][:C]).astype(BF16) for h in heads]
        o2 = [jnp.dot(a_scr[slot, A_QK, h], jnp.concatenate([zeros_v, vnb[h]], axis=0),
                      preferred_element_type=F32) for h in heads]
        ds = [lax.dot_general(a_scr[slot, A_KG, h], vnb[h], (((0,), (0,)), ((), ())),
                              preferred_element_type=F32) for h in heads]
        yield
        for h in heads:
            egl_c = col(SC_EGL, h)
            state[h] = state[h] * jnp.concatenate([egl_c, egl_c], axis=0) + ds[h]
            o_ref[r0:r0 + C, h * GDN_DV:(h + 1) * GDN_DV] = r[h][C:] * col(SC_EGC, h) + o2[h]
        yield

    n_chunks = tile // C
    state = [state_ref[h] for h in heads]
    gate = gate_proj()
    for i, _ in enumerate(intra(0)):
        if i in (2, 5):
            next(gate, None)
    for c in range(n_chunks):
        a = intra(c + 1) if c + 1 < n_chunks else gate
        b = inter(c, state)
        for who in "abaaabaaaba":
            next(a if who == "a" else b, None)
    for _ in gate:
        pass
    for h in heads:
        state_ref[h] = state[h]

    gated = []
    for h in range(GDN_HEADS):
        oh = o_ref[:, h * GDN_DV:(h + 1) * GDN_DV]
        ms = jnp.mean(oh * oh, axis=-1, keepdims=True)
        on = oh * lax.rsqrt(ms + RMS_EPS) * ng_ref[...]
        zh = z_scr[:, h * GDN_DV:(h + 1) * GDN_DV].astype(F32)
        gated.append((on * _silu(zh)).astype(BF16))
    y = jnp.dot(jnp.concatenate(gated, axis=1), wout_ref[...], preferred_element_type=F32)
    out_ref[0] = _layer_norm(ALPHA * x_ref[0] + y, lng_ref[...], lnb_ref[...])


def _gdn_core(q, k, v, scal, x, w_z, w_out, norm_g, ln_g, ln_b, *, tile):
    B, S, D = x.shape
    grid = (B, S // tile)
    tok = lambda w: pl.BlockSpec((1, tile, w), lambda b, s: (b, s, 0))
    return pl.pallas_call(
        functools.partial(_gdn_core_kernel, tile=tile),
        grid=grid,
        in_specs=[tok(GDN_QK_W), tok(GDN_QK_W), tok(GDN_V_W), tok(128), tok(D),
                  _resident(w_z.shape), _resident(w_out.shape), _resident(norm_g.shape),
                  _resident(ln_g.shape), _resident(ln_b.shape)],
        out_specs=tok(D),
        out_shape=jax.ShapeDtypeStruct((B, S, D), F32),
        scratch_shapes=[pltpu.VMEM((GDN_HEADS, GDN_DK, GDN_DV), F32),
                        pltpu.VMEM((tile, GDN_V_W), F32),
                        pltpu.VMEM((2, 4, GDN_HEADS, CHUNK, 2 * CHUNK), BF16),
                        pltpu.VMEM((tile, GDN_V_W), BF16)],
        compiler_params=pltpu.CompilerParams(
            dimension_semantics=("arbitrary", "arbitrary"),
            vmem_limit_bytes=V7X_VMEM_LIMIT),
        name="gdn_core",
    )(q, k, v, scal, x, w_z, w_out, norm_g, ln_g, ln_b)


def _att_in_kernel(h_ref, w_ref, cos_ref, sin_ref, *refs, tile):
    out_refs, stage_ref = refs[:-1], refs[-1]
    q_refs, k_refs, v_refs, z_ref = out_refs[0:3], out_refs[3:6], out_refs[6:9], out_refs[9]
    hb = h_ref[0].astype(BF16)
    cos = cos_ref[...]
    sin_s = sin_ref[...]

    def rope(t, scale):
        outs = []
        for h in range(HEADS_PER_GROUP):
            th = t[:, h * HEAD_DIM:(h + 1) * HEAD_DIM]
            r = th * cos + pltpu.roll(th, HEAD_DIM // 2, axis=1) * sin_s
            outs.append(r * scale if scale != 1.0 else r)
        return jnp.concatenate(outs, axis=1)

    def emit(out_ref, val, d):
        if d == 1:
            out_ref[0] = val.astype(BF16)
            return
        for h in range(HEADS_PER_GROUP):
            stage_ref[h] = val[:, h * HEAD_DIM:(h + 1) * HEAD_DIM]
        for r in range(d):
            for h in range(HEADS_PER_GROUP):
                c0 = r * GROUP_W + h * HEAD_DIM
                out_ref[0, :, c0:c0 + HEAD_DIM] = (
                    stage_ref[h, pl.ds(r, tile // d, stride=d), :].astype(BF16))

    for g, (_, d) in enumerate(DIL_GROUPS):
        c0 = g * GROUP_W
        qg = jnp.dot(hb, w_ref[:, c0:c0 + GROUP_W], preferred_element_type=F32)
        emit(q_refs[g], rope(qg, HEAD_DIM ** -0.5), d)
        c0 = ATT_Q_W + g * GROUP_W
        kg = jnp.dot(hb, w_ref[:, c0:c0 + GROUP_W], preferred_element_type=F32)
        emit(k_refs[g], rope(kg, 1.0), d)
        c0 = 2 * ATT_Q_W + g * GROUP_W
        emit(v_refs[g], jnp.dot(hb, w_ref[:, c0:c0 + GROUP_W], preferred_element_type=F32), d)
    c0 = 3 * ATT_Q_W
    z_ref[0] = jnp.dot(hb, w_ref[:, c0:c0 + GROUP_W], preferred_element_type=F32).astype(BF16)


def _att_in(h, w_att, cos, sin_s, *, tile):
    B, S, D = h.shape
    grid = (B, S // tile)
    tok = lambda w: pl.BlockSpec((1, tile, w), lambda b, s: (b, s, 0))
    tab = pl.BlockSpec((tile, HEAD_DIM), lambda b, s: (s, 0))
    dil = lambda d: pl.BlockSpec((1, tile // d, d * GROUP_W), lambda b, s: (b, s, 0))
    dil_shape = lambda d: jax.ShapeDtypeStruct((B, S // d, d * GROUP_W), BF16)
    ds_ = [d for _, d in DIL_GROUPS]
    return pl.pallas_call(
        functools.partial(_att_in_kernel, tile=tile),
        grid=grid,
        in_specs=[tok(D), _resident(w_att.shape), tab, tab],
        out_specs=[dil(d) for d in ds_] * 3 + [tok(GROUP_W)],
        out_shape=[dil_shape(d) for d in ds_] * 3 + [jax.ShapeDtypeStruct((B, S, GROUP_W), BF16)],
        scratch_shapes=[pltpu.VMEM((HEADS_PER_GROUP, tile, HEAD_DIM), F32)],
        compiler_params=pltpu.CompilerParams(
            dimension_semantics=("arbitrary", "arbitrary"),
            vmem_limit_bytes=V7X_VMEM_LIMIT),
        name="att_in",
    )(h, w_att, cos, sin_s)


def _dil_att_kernel(q_ref, k_ref, v_ref, o_ref, lse_ref, kbuf, vbuf, *, dilation, steps):
    blk = ATT_BLOCK
    m = ATT_SPAN // (blk * dilation)
    rows = m * blk
    n = pl.program_id(1)

    @pl.when(n == 0)
    def _():
        kbuf[0:blk, :] = jnp.zeros((blk, kbuf.shape[1]), kbuf.dtype)
        vbuf[0:blk, :] = jnp.zeros((blk, vbuf.shape[1]), vbuf.dtype)

    kbuf[blk:blk + rows, :] = k_ref[0]
    vbuf[blk:blk + rows, :] = v_ref[0]

    qi = lax.broadcasted_iota(jnp.int32, (blk, 2 * blk), 0)
    kj = lax.broadcasted_iota(jnp.int32, (blk, 2 * blk), 1)
    off = qi + blk - kj
    band = (off >= 0) & (off <= steps)
    first_lo = jnp.where(n > 0, 0, blk)
    band_first = band & (kj >= first_lo)
    lane = lax.broadcasted_iota(jnp.int32, (blk, HEAD_DIM), 1)

    for i in range(m):
        mask = band_first if i == 0 else band
        for r in range(dilation):
            lse_tile = jnp.zeros((blk, HEAD_DIM), F32)
            for h in range(HEADS_PER_GROUP):
                c0 = r * GROUP_W + h * HEAD_DIM
                qt = q_ref[0, i * blk:(i + 1) * blk, c0:c0 + HEAD_DIM]
                kt = kbuf[i * blk:(i + 2) * blk, c0:c0 + HEAD_DIM]
                vt = vbuf[i * blk:(i + 2) * blk, c0:c0 + HEAD_DIM]
                s = lax.dot_general(qt, kt, (((1,), (1,)), ((), ())), preferred_element_type=F32)
                s = jnp.where(mask, s, NEG_BIG)
                mx = jnp.max(s, axis=-1, keepdims=True)
                p = jnp.exp(s - mx)
                den = jnp.sum(p, axis=-1, keepdims=True)
                o = jnp.dot(p.astype(BF16), vt, preferred_element_type=F32) / den
                o_ref[0, i * blk:(i + 1) * blk, c0:c0 + HEAD_DIM] = o.astype(BF16)
                lse_tile = jnp.where(lane == h, mx + jnp.log(den), lse_tile)
            lse_ref[0, i * blk:(i + 1) * blk, r * HEAD_DIM:(r + 1) * HEAD_DIM] = lse_tile

    kbuf[0:blk, :] = kbuf[rows:rows + blk, :]
    vbuf[0:blk, :] = vbuf[rows:rows + blk, :]


def _dil_att(q, k, v, *, window, dilation):
    d = dilation
    B = q.shape[0]
    S = q.shape[1] * d
    rows = ATT_SPAN // d
    wide = d * GROUP_W
    grid = (B, S // ATT_SPAN)
    spec = pl.BlockSpec((1, rows, wide), lambda b, n: (b, n, 0))
    lse_spec = pl.BlockSpec((1, rows, d * HEAD_DIM), lambda b, n: (b, n, 0))
    return pl.pallas_call(
        functools.partial(_dil_att_kernel, dilation=d, steps=window // d),
        grid=grid,
        in_specs=[spec, spec, spec],
        out_specs=[spec, lse_spec],
        out_shape=[jax.ShapeDtypeStruct((B, S // d, wide), BF16),
                   jax.ShapeDtypeStruct((B, S // d, d * HEAD_DIM), F32)],
        scratch_shapes=[pltpu.VMEM((rows + ATT_BLOCK, wide), BF16),
                        pltpu.VMEM((rows + ATT_BLOCK, wide), BF16)],
        compiler_params=pltpu.CompilerParams(
            dimension_semantics=("arbitrary", "arbitrary"),
            vmem_limit_bytes=V7X_VMEM_LIMIT),
        name=f"dil_att_d{d}",
    )(q, k, v)


def _att_out_kernel(o0_ref, o1_ref, o2_ref, l0_ref, l1_ref, l2_ref, z_ref, h_ref,
                    wout_ref, lng_ref, lnb_ref, out_ref, o_scr, l_scr, *, tile):
    for g, ((_, d), o_ref, l_ref) in enumerate(zip(DIL_GROUPS, (o0_ref, o1_ref, o2_ref),
                                                   (l0_ref, l1_ref, l2_ref))):
        for r in range(d):
            rows = pl.ds(r, tile // d, stride=d) if d > 1 else slice(None)
            for h in range(HEADS_PER_GROUP):
                c0 = r * GROUP_W + h * HEAD_DIM
                o_scr[g * HEADS_PER_GROUP + h, rows, :] = o_ref[0, :, c0:c0 + HEAD_DIM].astype(F32)
            l_scr[g, rows, :] = l_ref[0, :, r * HEAD_DIM:(r + 1) * HEAD_DIM]
    lses = [l_scr[g] for g in range(N_GROUPS)]
    mx = jnp.maximum(jnp.maximum(lses[0], lses[1]), lses[2])
    es = [jnp.exp(l - mx) for l in lses]
    inv = 1.0 / (es[0] + es[1] + es[2])
    wts = [e * inv for e in es]
    merged = []
    for h in range(HEADS_PER_GROUP):
        sl = slice(h * HEAD_DIM, (h + 1) * HEAD_DIM)
        acc = None
        for g in range(N_GROUPS):
            t = wts[g][:, h:h + 1] * o_scr[g * HEADS_PER_GROUP + h]
            acc = t if acc is None else acc + t
        zh = z_ref[0, :, sl].astype(F32)
        merged.append((acc * _silu(zh)).astype(BF16))
    y = jnp.dot(jnp.concatenate(merged, axis=1), wout_ref[...], preferred_element_type=F32)
    out_ref[0] = _layer_norm(ALPHA * h_ref[0] + y, lng_ref[...], lnb_ref[...])


def _att_out(os_, lses, z, h, w_out, ln_g, ln_b, *, tile):
    B, S, D = h.shape
    grid = (B, S // tile)
    tok = lambda w: pl.BlockSpec((1, tile, w), lambda b, s: (b, s, 0))
    dil = lambda d, w: pl.BlockSpec((1, tile // d, d * w), lambda b, s: (b, s, 0))
    ds_ = [d for _, d in DIL_GROUPS]
    return pl.pallas_call(
        functools.partial(_att_out_kernel, tile=tile),
        grid=grid,
        in_specs=[dil(d, GROUP_W) for d in ds_] + [dil(d, HEAD_DIM) for d in ds_] + [
                  tok(GROUP_W), tok(D),
                  _resident(w_out.shape), _resident(ln_g.shape), _resident(ln_b.shape)],
        out_specs=tok(D),
        out_shape=jax.ShapeDtypeStruct((B, S, D), F32),
        scratch_shapes=[pltpu.VMEM((N_GROUPS * HEADS_PER_GROUP, tile, HEAD_DIM), F32),
                        pltpu.VMEM((N_GROUPS, tile, HEAD_DIM), F32)],
        compiler_params=pltpu.CompilerParams(
            dimension_semantics=("arbitrary", "arbitrary"),
            vmem_limit_bytes=V7X_VMEM_LIMIT),
        name="att_out",
    )(*os_, *lses, z, h, w_out, ln_g, ln_b)


def _rope_tables(seq_len):
    inv = 1.0 / (ROPE_THETA ** (jnp.arange(0, HEAD_DIM, 2, dtype=F32) / HEAD_DIM))
    ang = jnp.arange(seq_len, dtype=F32)[:, None] * inv[None, :]
    ang = jnp.concatenate([ang, ang], -1)
    sign = jnp.where(jnp.arange(HEAD_DIM) < HEAD_DIM // 2, -1.0, 1.0).astype(F32)
    return jnp.cos(ang), jnp.sin(ang) * sign[None, :]


def _pad_lanes(v, width=128):
    return jnp.zeros((1, width), F32).at[0, :v.shape[0]].set(v.astype(F32))


def kernel(x, ln_g, ln_b, gdn_w_in, gdn_conv_w, gdn_a_log, gdn_dt_bias, gdn_norm_g, gdn_w_out,
           kv_w, att_w_in, att_w_out):
    B, S, D = x.shape
    w_in = gdn_w_in[0]
    w_main = w_in[:, :GDN_CONV_W].astype(BF16)
    n_main = GDN_CONV_W + GDN_V_W
    w_z = w_in[:, GDN_CONV_W:n_main].astype(BF16)
    w_ba = jnp.zeros((D, 256), F32)
    w_ba = w_ba.at[:, 0:GDN_HEADS].set(w_in[:, n_main:n_main + GDN_HEADS])
    w_ba = w_ba.at[:, 128:128 + GDN_HEADS].set(w_in[:, n_main + GDN_HEADS:]).astype(BF16)
    q, k, v, scal = _gdn_in(x, w_main, w_ba, gdn_conv_w[0], _pad_lanes(gdn_a_log[0]),
                            _pad_lanes(gdn_dt_bias[0]), tile=512)
    h1 = _gdn_core(q, k, v, scal, x, w_z, gdn_w_out[0].astype(BF16), gdn_norm_g[0][None, :],
                   ln_g[0][None, :], ln_b[0][None, :], tile=512)
    w_att = jnp.concatenate([att_w_in[0][:, :ATT_Q_W], kv_w, att_w_in[0][:, ATT_Q_W:]], axis=1).astype(BF16)
    cos, sin_s = _rope_tables(S)
    outs = _att_in(h1, w_att, cos, sin_s, tile=512)
    qs, ks, vs, za = outs[0:3], outs[3:6], outs[6:9], outs[9]
    os_, lses = [], []
    for g, (window, dilation) in enumerate(DIL_GROUPS):
        o, lse = _dil_att(qs[g], ks[g], vs[g], window=window, dilation=dilation)
        os_.append(o)
        lses.append(lse)
    return _att_out(os_, lses, za, h1, att_w_out[0].astype(BF16), ln_g[1][None, :], ln_b[1][None, :],
                    tile=512)
```
